```python
import jax, jax.numpy as jnp
from jax import lax
import numpy as np

D_MODEL = 1024
BATCH = 4
SEQ = 8192
DEPTH = 2

HEAD_DIM = 64
GROUP_WIDTH = D_MODEL // 2
GROUP_HEADS = GROUP_WIDTH // HEAD_DIM
RW_HEADS = GROUP_HEADS
RW_DIM = GROUP_WIDTH
DECAY_LORA = 64
AAA_LORA = 64
GATE_LORA = 128
RW_GN_EPS = 64e-5
MLA_HEADS = GROUP_HEADS
MLA_NOPE = HEAD_DIM
MLA_ROPE = HEAD_DIM // 2
MLA_V = HEAD_DIM
MLA_Q_RANK = D_MODEL // 4
MLA_KV_RANK = D_MODEL // 8
FOX_HEADS = GROUP_HEADS
FOX_DIM = GROUP_WIDTH
MOBA_HEADS = GROUP_HEADS
MOBA_DIM = GROUP_WIDTH
MOBA_BLOCK = 256
MOBA_TOPK = 3
MOBA_Q_CHUNK = 32
Q_BLOCK = 128
ROPE_THETA = 10000.0
NORM_EPS = 1e-6
D_FF = 256 * ((8 * D_MODEL // 3 + 255) // 256)
N_EXPERTS = 8
TOP_K = 2
D_EXPERT = 7 * D_MODEL // 2

RW_SIZES = (RW_DIM, RW_DIM, RW_DIM, DECAY_LORA, AAA_LORA, GATE_LORA)
RW_COLS = sum(RW_SIZES)
MLA_SIZES = (MLA_Q_RANK, MLA_KV_RANK, MLA_ROPE)
IN0_COLS = RW_COLS + sum(MLA_SIZES)
FOX_SIZES = (FOX_DIM, FOX_DIM, FOX_DIM, FOX_HEADS)
FOX_COLS = sum(FOX_SIZES)
MOBA_SIZES = (MOBA_DIM, MOBA_DIM, MOBA_DIM)
IN1_COLS = FOX_COLS + sum(MOBA_SIZES)
MIX0_OUT = RW_DIM + MLA_HEADS * MLA_V
MIX1_OUT = FOX_DIM + MOBA_DIM

kernel_name = 'hybrid_rwkv7_mla_fox_moba_moe'


def split_cols(t, sizes):
    return jnp.split(t, [int(i) for i in np.cumsum(sizes)[:-1]], axis=-1)


def to_heads(t, n_heads):
    b, s, _ = t.shape
    return t.reshape(b, s, n_heads, -1)


def rmsnorm(x, g, eps=NORM_EPS):
    xf = x.astype(jnp.float32)
    xf = xf * lax.rsqrt(jnp.mean(xf * xf, axis=-1, keepdims=True) + eps)
    return (xf * g.astype(jnp.float32)).astype(x.dtype)


def rope_tables(seq_len, dim):
    inv_freq = ROPE_THETA ** (-jnp.arange(0, dim, 2, dtype=jnp.float32) / dim)
    ang = jnp.arange(seq_len, dtype=jnp.float32)[:, None] * inv_freq[None, :]
    return jnp.cos(ang), jnp.sin(ang)


def apply_rope(x, cos_r, sin_r):
    xf = x.astype(jnp.float32)
    x1, x2 = jnp.split(xf, 2, axis=-1)
    c, s = cos_r[:, None, :], sin_r[:, None, :]
    return jnp.concatenate([x1 * c - x2 * s, x2 * c + x1 * s], axis=-1).astype(x.dtype)


def swiglu(t, w_gate, w_up, w_down):
    return (jax.nn.silu(t @ w_gate) * (t @ w_up)) @ w_down


def block_causal_attention(q, k, v, scale, log_decay_cum=None):
    b, h, s, _ = q.shape
    k_pos = jnp.arange(s)

    def one_block(i):
        start = i * Q_BLOCK
        qb = lax.dynamic_slice_in_dim(q, start, Q_BLOCK, axis=2)
        logits = jnp.einsum('bhqd,bhkd->bhqk', qb, k).astype(jnp.float32) * scale
        q_pos = start + jnp.arange(Q_BLOCK)
        if log_decay_cum is not None:
            dq = lax.dynamic_slice_in_dim(log_decay_cum, start, Q_BLOCK, axis=2)
            logits = logits + dq[..., :, None] - log_decay_cum[..., None, :]
        logits = jnp.where(k_pos[None, :] <= q_pos[:, None], logits, -jnp.inf)
        p = jax.nn.softmax(logits, axis=-1).astype(v.dtype)
        return jnp.einsum('bhqk,bhkd->bhqd', p, v)

    out = lax.map(one_block, jnp.arange(s // Q_BLOCK))
    return out.transpose(1, 2, 0, 3, 4).reshape(b, h, s, -1)


def rwkv7_time_mix(r, k, v, w_down, a_down, g_down, w0, w2, a0, a2, g2,
                   k_k, k_a, r_k, lnx_g, lnx_b):
    out_dtype = r.dtype
    f32 = jnp.float32
    r, k, v, w_down, a_down, g_down = (t.astype(f32) for t in (r, k, v, w_down, a_down, g_down))
    b, s, c = r.shape
    h, n = RW_HEADS, HEAD_DIM
    log_w = -jax.nn.softplus(-(w0 + jnp.tanh(w_down) @ w2)) - 0.5
    decay = jnp.exp(-jnp.exp(log_w))
    a = jax.nn.sigmoid(a0 + a_down @ a2)
    g = jax.nn.sigmoid(g_down) @ g2
    kk = (k * k_k).reshape(b, s, h, n)
    kk = kk / jnp.maximum(jnp.linalg.norm(kk, axis=-1, keepdims=True), 1e-12)
    k = k * (1.0 + (a - 1.0) * k_a)
    r_h, w_h, k_h, v_h, a_h = (t.reshape(b, s, h, n) for t in (r, decay, k, v, a))
    b_h = kk * a_h
    xs = tuple(jnp.moveaxis(t, 1, 0) for t in (r_h, w_h, k_h, v_h, kk, b_h))

    def step(state, inp):
        r_t, w_t, k_t, v_t, kk_t, b_t = inp
        sa = jnp.einsum('bhvk,bhk->bhv', state, -kk_t)
        state = (state * w_t[:, :, None, :] + sa[..., None] * b_t[:, :, None, :]
                 + v_t[..., None] * k_t[:, :, None, :])
        return state, jnp.einsum('bhvk,bhk->bhv', state, r_t)

    state0 = jnp.zeros((b, h, n, n), f32)
    _, y = lax.scan(step, state0, xs)
    y = jnp.moveaxis(y, 0, 1)
    mu = jnp.mean(y, axis=-1, keepdims=True)
    var = jnp.mean(jnp.square(y - mu), axis=-1, keepdims=True)
    y = ((y - mu) * lax.rsqrt(var + RW_GN_EPS)).reshape(b, s, c) * lnx_g + lnx_b
    bonus = jnp.sum(r_h * k_h * r_k, axis=-1, keepdims=True) * v_h
    y = (y + bonus.reshape(b, s, c)) * g
    return y.astype(out_dtype)


def mla_attention(q_lat, kv_lat, k_rope, q_norm, w_uq, kv_norm, w_ukv, cos_r, sin_r):
    b, s, _ = q_lat.shape
    h = MLA_HEADS
    q = to_heads(rmsnorm(q_lat, q_norm) @ w_uq, h)
    q_nope, q_pe = q[..., :MLA_NOPE], apply_rope(q[..., MLA_NOPE:], cos_r, sin_r)
    kv = to_heads(rmsnorm(kv_lat, kv_norm) @ w_ukv, h)
    k_nope, v = kv[..., :MLA_NOPE], kv[..., MLA_NOPE:]
    k_pe = apply_rope(k_rope[:, :, None, :], cos_r, sin_r)
    q = jnp.concatenate([q_nope, q_pe], axis=-1)
    k = jnp.concatenate([k_nope, jnp.broadcast_to(k_pe, (b, s, h, MLA_ROPE))], axis=-1)
    out = block_causal_attention(q.transpose(0, 2, 1, 3), k.transpose(0, 2, 1, 3),
                                 v.transpose(0, 2, 1, 3), (MLA_NOPE + MLA_ROPE) ** -0.5)
    return out.transpose(0, 2, 1, 3).reshape(b, s, h * MLA_V)


def forgetting_attention(q, k, v, f_logit, b_f):
    b, s, _ = q.shape
    h = FOX_HEADS
    log_f = jax.nn.log_sigmoid(f_logit.astype(jnp.float32) + b_f.astype(jnp.float32))
    d_cum = jnp.cumsum(log_f, axis=1).transpose(0, 2, 1)
    qh, kh, vh = (to_heads(t, h).transpose(0, 2, 1, 3) for t in (q, k, v))
    out = block_causal_attention(qh, kh, vh, HEAD_DIM ** -0.5, d_cum)
    return out.transpose(0, 2, 1, 3).reshape(b, s, h * HEAD_DIM)


def moba_attention(q, k, v, cos_r, sin_r):
    b, s, _ = q.shape
    h, d = MOBA_HEADS, HEAD_DIM
    qh = apply_rope(to_heads(q, h), cos_r, sin_r).transpose(0, 2, 1, 3)
    kh = apply_rope(to_heads(k, h), cos_r, sin_r).transpose(0, 2, 1, 3)
    vh = to_heads(v, h).transpose(0, 2, 1, 3)
    nb = -(-s // MOBA_BLOCK)
    pad = nb * MOBA_BLOCK - s
    kp = jnp.pad(kh, ((0, 0), (0, 0), (0, pad), (0, 0)))
    vp = jnp.pad(vh, ((0, 0), (0, 0), (0, pad), (0, 0)))
    k_blocks = kp.reshape(b, h, nb, MOBA_BLOCK, d)
    v_blocks = vp.reshape(b, h, nb, MOBA_BLOCK, d)
    k_mean = jnp.mean(k_blocks.astype(jnp.float32), axis=3).astype(kh.dtype)
    n_sel = min(MOBA_TOPK, nb)
    n_sel_keys = n_sel * MOBA_BLOCK
    scale = d ** -0.5
    b_idx = jnp.arange(b)[:, None, None, None]
    h_idx = jnp.arange(h)[None, :, None, None]
    blk_ids = jnp.arange(nb)
    offs = jnp.arange(MOBA_BLOCK)

    def one_chunk(i):
        start = i * MOBA_Q_CHUNK
        qc = lax.dynamic_slice_in_dim(qh, start, MOBA_Q_CHUNK, axis=2)
        q_pos = start + jnp.arange(MOBA_Q_CHUNK)
        own = start // MOBA_BLOCK
        gate = jnp.einsum('bhqd,bhnd->bhqn', qc, k_mean).astype(jnp.float32)
        gate = jnp.where(blk_ids < own, gate, -jnp.inf)
        _, sel = lax.top_k(gate, n_sel)
        sel_valid = sel < own
        k_sel = k_blocks[b_idx, h_idx, sel]
        v_sel = v_blocks[b_idx, h_idx, sel]
        s_sel = jnp.einsum('bhqd,bhqjld->bhqjl', qc, k_sel).astype(jnp.float32) * scale
        s_sel = jnp.where(sel_valid[..., None], s_sel, -jnp.inf)
        k_own = lax.dynamic_slice_in_dim(kp, own * MOBA_BLOCK, MOBA_BLOCK, axis=2)
        v_own = lax.dynamic_slice_in_dim(vp, own * MOBA_BLOCK, MOBA_BLOCK, axis=2)
        s_own = jnp.einsum('bhqd,bhld->bhql', qc, k_own).astype(jnp.float32) * scale
        s_own = jnp.where((own * MOBA_BLOCK + offs)[None, :] <= q_pos[:, None], s_own, -jnp.inf)
        logits = jnp.concatenate([s_sel.reshape(b, h, MOBA_Q_CHUNK, n_sel_keys), s_own], axis=-1)
        p = jax.nn.softmax(logits, axis=-1).astype(vh.dtype)
        p_sel = p[..., :n_sel_keys].reshape(b, h, MOBA_Q_CHUNK, n_sel, MOBA_BLOCK)
        p_own = p[..., n_sel_keys:]
        return (jnp.einsum('bhqjl,bhqjld->bhqd', p_sel, v_sel)
                + jnp.einsum('bhql,bhld->bhqd', p_own, v_own))

    out = lax.map(one_chunk, jnp.arange(s // MOBA_Q_CHUNK))
    out = out.transpose(1, 2, 0, 3, 4).reshape(b, h, s, d)
    return out.transpose(0, 2, 1, 3).reshape(b, s, h * d)


def moe_swiglu(t, router, w_gate, w_up, w_down):
    logits = (t @ router).astype(jnp.float32)
    top_logits, top_idx = lax.top_k(logits, TOP_K)
    gates = jax.nn.softmax(top_logits, axis=-1)
    combine = jnp.sum(jax.nn.one_hot(top_idx, N_EXPERTS, dtype=jnp.float32) * gates[..., None], axis=1)
    out = jnp.zeros_like(t)
    for e in range(N_EXPERTS):
        out = out + combine[:, e:e + 1].astype(t.dtype) * swiglu(t, w_gate[e], w_up[e], w_down[e])
    return out


def even_layer(h, norm_mix, w_in, shift_mu, rw_w0, rw_w2, rw_a0, rw_a2, rw_g2, rw_kk, rw_ka,
               rw_rk, rw_lnx_g, rw_lnx_b, mla_qnorm, mla_wuq, mla_kvnorm, mla_wukv, w_out,
               norm_ffn, ffn_wg, ffn_wu, ffn_wd, cos_r, sin_r):
    u = rmsnorm(h, norm_mix)
    proj = u @ w_in
    rw_cols, mla_cols = proj[..., :RW_COLS], proj[..., RW_COLS:]
    rw_prev = jnp.pad(rw_cols[:, :-1], ((0, 0), (1, 0), (0, 0)))
    rw_cols = rw_cols + shift_mu * (rw_prev - rw_cols)
    y_a = rwkv7_time_mix(*split_cols(rw_cols, RW_SIZES), rw_w0, rw_w2, rw_a0, rw_a2, rw_g2,
                         rw_kk, rw_ka, rw_rk, rw_lnx_g, rw_lnx_b)
    y_b = mla_attention(*split_cols(mla_cols, MLA_SIZES), mla_qnorm, mla_wuq, mla_kvnorm,
                        mla_wukv, cos_r, sin_r)
    h = h + jnp.concatenate([y_a, y_b], axis=-1) @ w_out
    return h + swiglu(rmsnorm(h, norm_ffn), ffn_wg, ffn_wu, ffn_wd)


def odd_layer(h, norm_mix, w_in, fox_bf, w_out, norm_ffn, router, moe_wg, moe_wu, moe_wd,
              cos_r, sin_r):
    b, s, d = h.shape
    u = rmsnorm(h, norm_mix)
    proj = u @ w_in
    fox_cols, moba_cols = proj[..., :FOX_COLS], proj[..., FOX_COLS:]
    y_c = forgetting_attention(*split_cols(fox_cols, FOX_SIZES), fox_bf)
    y_d = moba_attention(*split_cols(moba_cols, MOBA_SIZES), cos_r, sin_r)
    h = h + jnp.concatenate([y_c, y_d], axis=-1) @ w_out
    t = rmsnorm(h, norm_ffn).reshape(b * s, d)
    return h + moe_swiglu(t, router, moe_wg, moe_wu, moe_wd).reshape(b, s, d)


def setup_inputs(seed: int = 0) -> dict:
    key = jax.random.key(seed)
    ks = iter(jax.random.split(key, 40))
    D = D_MODEL

    def nrm(shape, scale):
        return scale * jax.random.normal(next(ks), shape, jnp.float32)

    def gain(n):
        return 1.0 + nrm((n,), 0.05)

    return {
        'x': nrm((BATCH, SEQ, D), 1.0),
        'norm_mix_0': gain(D),
        'w_in_0': nrm((D, IN0_COLS), D ** -0.5),
        'shift_mu_0': jax.random.uniform(next(ks), (RW_COLS,), jnp.float32, 0.2, 0.8),
        'rw_w0_0': jax.random.uniform(next(ks), (RW_DIM,), jnp.float32, -6.0, -1.0),
        'rw_w2_0': nrm((DECAY_LORA, RW_DIM), DECAY_LORA ** -0.5),
        'rw_a0_0': nrm((RW_DIM,), 0.1),
        'rw_a2_0': nrm((AAA_LORA, RW_DIM), AAA_LORA ** -0.5),
        'rw_g2_0': nrm((GATE_LORA, RW_DIM), GATE_LORA ** -0.5),
        'rw_kk_0': 0.85 + nrm((RW_DIM,), 0.05),
        'rw_ka_0': 1.0 + nrm((RW_DIM,), 0.05),
        'rw_rk_0': nrm((RW_HEADS, HEAD_DIM), 0.1),
        'rw_lnx_g_0': gain(RW_DIM),
        'rw_lnx_b_0': nrm((RW_DIM,), 0.02),
        'mla_qnorm_0': gain(MLA_Q_RANK),
        'mla_wuq_0': nrm((MLA_Q_RANK, MLA_HEADS * (MLA_NOPE + MLA_ROPE)), MLA_Q_RANK ** -0.5),
        'mla_kvnorm_0': gain(MLA_KV_RANK),
        'mla_wukv_0': nrm((MLA_KV_RANK, MLA_HEADS * (MLA_NOPE + MLA_V)), MLA_KV_RANK ** -0.5),
        'w_out_0': nrm((MIX0_OUT, D), MIX0_OUT ** -0.5),
        'norm_ffn_0': gain(D),
        'ffn_wg_0': nrm((D, D_FF), D ** -0.5),
        'ffn_wu_0': nrm((D, D_FF), D ** -0.5),
        'ffn_wd_0': nrm((D_FF, D), D_FF ** -0.5),
        'norm_mix_1': gain(D),
        'w_in_1': nrm((D, IN1_COLS), D ** -0.5),
        'fox_bf_1': 3.0 + nrm((FOX_HEADS,), 0.5),
        'w_out_1': nrm((MIX1_OUT, D), MIX1_OUT ** -0.5),
        'norm_ffn_1': gain(D),
        'router_1': nrm((D, N_EXPERTS), D ** -0.5),
        'moe_wg_1': nrm((N_EXPERTS, D, D_EXPERT), D ** -0.5),
        'moe_wu_1': nrm((N_EXPERTS, D, D_EXPERT), D ** -0.5),
        'moe_wd_1': nrm((N_EXPERTS, D_EXPERT, D), D_EXPERT ** -0.5),
        'final_norm': gain(D),
    }


def reference(x, norm_mix_0, w_in_0, shift_mu_0, rw_w0_0, rw_w2_0, rw_a0_0, rw_a2_0, rw_g2_0,
              rw_kk_0, rw_ka_0, rw_rk_0, rw_lnx_g_0, rw_lnx_b_0, mla_qnorm_0, mla_wuq_0,
              mla_kvnorm_0, mla_wukv_0, w_out_0, norm_ffn_0, ffn_wg_0, ffn_wu_0, ffn_wd_0,
              norm_mix_1, w_in_1, fox_bf_1, w_out_1, norm_ffn_1, router_1, moe_wg_1, moe_wu_1,
              moe_wd_1, final_norm):
    seq_len = x.shape[1]
    rope_mla = rope_tables(seq_len, MLA_ROPE)
    rope_full = rope_tables(seq_len, HEAD_DIM)
    layer_params = (
        (norm_mix_0, w_in_0, shift_mu_0, rw_w0_0, rw_w2_0, rw_a0_0, rw_a2_0, rw_g2_0, rw_kk_0,
         rw_ka_0, rw_rk_0, rw_lnx_g_0, rw_lnx_b_0, mla_qnorm_0, mla_wuq_0, mla_kvnorm_0,
         mla_wukv_0, w_out_0, norm_ffn_0, ffn_wg_0, ffn_wu_0, ffn_wd_0),
        (norm_mix_1, w_in_1, fox_bf_1, w_out_1, norm_ffn_1, router_1, moe_wg_1, moe_wu_1,
         moe_wd_1),
    )
    h = x
    for layer in range(DEPTH):
        if layer % 2 == 0:
            h = even_layer(h, *layer_params[layer], *rope_mla)
        else:
            h = odd_layer(h, *layer_params[layer], *rope_full)
    return rmsnorm(h, final_norm)
```

```python
import functools

import numpy as np
import jax
import jax.numpy as jnp
from jax import lax
from jax.experimental import pallas as pl
from jax.experimental.pallas import tpu as pltpu

F32 = jnp.float32
BF16 = jnp.bfloat16

HEAD_DIM = 64
N_HEADS = 8
GROUP_WIDTH = N_HEADS * HEAD_DIM
HEAD_LANES = 128
MLA_ROPE = 32
MLA_Q_RANK = 256
MLA_KV_RANK = 128
DECAY_LORA = 64
AAA_LORA = 64
GATE_LORA = 128
RW_COLS = 3 * GROUP_WIDTH + DECAY_LORA + AAA_LORA + GATE_LORA
RW_GN_EPS = 64e-5
MOBA_BLOCK = 256
MOBA_TOPK = 3
ROPE_THETA = 10000.0
NORM_EPS = 1e-6
N_EXPERTS = 8
NEG_BIG = -1e30

LANES = 128
VMEM_LIMIT_BYTES = 56 * 1024 * 1024

RW_CHUNK = 64
RW_GROUP_HEADS = 4
RW_GROUP_LANES = RW_GROUP_HEADS * HEAD_DIM


def _cparams(*semantics):
    return pltpu.CompilerParams(dimension_semantics=semantics, vmem_limit_bytes=VMEM_LIMIT_BYTES)


def _dot(a, b, prec=1):
    if prec == 6:
        return jnp.dot(a.astype(F32), b.astype(F32), preferred_element_type=F32,
                       precision=lax.Precision.HIGHEST)
    if prec == 1:
        return jnp.dot(a.astype(BF16), b.astype(BF16), preferred_element_type=F32)
    ah, al = _split(a)
    bh, bl = _split(b)
    return (jnp.dot(ah, bh, preferred_element_type=F32) + jnp.dot(ah, bl, preferred_element_type=F32)
            + jnp.dot(al, bh, preferred_element_type=F32))


def _split(x):
    if x.dtype == BF16:
        return x, jnp.zeros_like(x)
    hi = x.astype(BF16)
    return hi, (x - hi.astype(F32)).astype(BF16)


_NT = (((1,), (1,)), ((), ()))
_TN = (((0,), (0,)), ((), ()))


def _dot_general(a, b, dims, prec=1):
    if prec == 6:
        return lax.dot_general(a.astype(F32), b.astype(F32), dims, preferred_element_type=F32,
                               precision=lax.Precision.HIGHEST)
    if prec == 1:
        return lax.dot_general(a.astype(BF16), b.astype(BF16), dims, preferred_element_type=F32)
    ah, al = _split(a)
    bh, bl = _split(b)
    return (lax.dot_general(ah, bh, dims, preferred_element_type=F32)
            + lax.dot_general(ah, bl, dims, preferred_element_type=F32)
            + lax.dot_general(al, bh, dims, preferred_element_type=F32))


def _rms(x, g):
    return x * lax.rsqrt(jnp.mean(x * x, axis=-1, keepdims=True) + NORM_EPS) * g


def _softplus(x):
    return jnp.maximum(x, 0.0) + jnp.log(1.0 + jnp.exp(-jnp.abs(x)))


def _sigmoid(x):
    return 1.0 / (1.0 + jnp.exp(-x))


def _row_spec(tm, n):
    return pl.BlockSpec((tm, n), lambda i: (i, 0))


def _full_spec(shape):
    nd = len(shape)
    return pl.BlockSpec(shape, lambda *_: (0,) * nd)


def _norm_proj_kernel(n_out, x_ref, g_ref, *refs):
    w_refs, o_refs = refs[:n_out], refs[n_out:]
    xn = _rms(x_ref[...], g_ref[...]).astype(BF16)
    for w_ref, o_ref in zip(w_refs, o_refs):
        o_ref[...] = jnp.dot(xn, w_ref[...], preferred_element_type=F32).astype(o_ref.dtype)


def norm_proj(x, g, weights, dtypes, *, tm=256, name="norm_proj"):
    t, d = x.shape
    n_out = len(weights)
    return pl.pallas_call(
        functools.partial(_norm_proj_kernel, n_out),
        grid=(t // tm,),
        in_specs=[_row_spec(tm, d), _full_spec((1, d))] + [_full_spec(w.shape) for w in weights],
        out_specs=[_row_spec(tm, w.shape[1]) for w in weights],
        out_shape=[jax.ShapeDtypeStruct((t, w.shape[1]), dt) for w, dt in zip(weights, dtypes)],
        compiler_params=_cparams("parallel"),
        name=name,
    )(x, g.reshape(1, d), *weights)


def _out_proj_kernel(y1_ref, y2_ref, w1_ref, w2_ref, r_ref, o_ref):
    o_ref[...] = (r_ref[...] + jnp.dot(y1_ref[...], w1_ref[...], preferred_element_type=F32)
                  + jnp.dot(y2_ref[...], w2_ref[...], preferred_element_type=F32))


def out_proj(y1, y2, w1, w2, resid, *, tm=512, name="out_proj"):
    t, d = resid.shape
    return pl.pallas_call(
        _out_proj_kernel,
        grid=(t // tm,),
        in_specs=[_row_spec(tm, y1.shape[1]), _row_spec(tm, y2.shape[1]),
                  _full_spec(w1.shape), _full_spec(w2.shape), _row_spec(tm, d)],
        out_specs=_row_spec(tm, d),
        out_shape=jax.ShapeDtypeStruct((t, d), F32),
        compiler_params=_cparams("parallel"),
        name=name,
    )(y1, y2, w1, w2, resid)


def _ffn_kernel(h_ref, g_ref, wg_ref, wu_ref, wd_ref, o_ref, xn_ref, acc_ref):
    f = pl.program_id(1)

    @pl.when(f == 0)
    def _():
        xn_ref[...] = _rms(h_ref[...], g_ref[...]).astype(BF16)
        acc_ref[...] = jnp.zeros_like(acc_ref)

    xn = xn_ref[...]
    gate = jnp.dot(xn, wg_ref[...], preferred_element_type=F32)
    up = jnp.dot(xn, wu_ref[...], preferred_element_type=F32)
    act = (gate * _sigmoid(gate) * up).astype(BF16)
    acc_ref[...] += jnp.dot(act, wd_ref[...], preferred_element_type=F32)

    @pl.when(f == pl.num_programs(1) - 1)
    def _():
        o_ref[...] = h_ref[...] + acc_ref[...]


def ffn(h, g, wg, wu, wd, *, tm=1024, tf=256, name="ffn"):
    t, d = h.shape
    dff = wg.shape[1]
    return pl.pallas_call(
        _ffn_kernel,
        grid=(t // tm, dff // tf),
        in_specs=[pl.BlockSpec((tm, d), lambda i, f: (i, 0)),
                  pl.BlockSpec((1, d), lambda i, f: (0, 0)),
                  pl.BlockSpec((d, tf), lambda i, f: (0, f)),
                  pl.BlockSpec((d, tf), lambda i, f: (0, f)),
                  pl.BlockSpec((tf, d), lambda i, f: (f, 0))],
        out_specs=pl.BlockSpec((tm, d), lambda i, f: (i, 0)),
        out_shape=jax.ShapeDtypeStruct((t, d), F32),
        scratch_shapes=[pltpu.VMEM((tm, d), BF16), pltpu.VMEM((tm, d), F32)],
        compiler_params=_cparams("parallel", "arbitrary"),
        name=name,
    )(h, g.reshape(1, d), wg, wu, wd)


def _moe_kernel(h_ref, g_ref, router_ref, wg_ref, wu_ref, wd_ref, fn_ref, o_ref,
                xn_ref, acc_ref, comb_ref, ce_ref):
    e = pl.program_id(1)
    f = pl.program_id(2)
    lane = lax.broadcasted_iota(jnp.int32, comb_ref.shape, 1)

    @pl.when((e == 0) & (f == 0))
    def _():
        t = _rms(h_ref[...], g_ref[...])
        xn_ref[...] = t.astype(BF16)
        acc_ref[...] = jnp.zeros_like(acc_ref)
        logits = _dot(t, router_ref[...], prec=6)
        logits = jnp.where(lane < N_EXPERTS, logits, NEG_BIG)
        m1 = jnp.max(logits, axis=-1, keepdims=True)
        i1 = jnp.min(jnp.where(logits == m1, lane, LANES), axis=-1, keepdims=True)
        rest = jnp.where(lane == i1, NEG_BIG, logits)
        m2 = jnp.max(rest, axis=-1, keepdims=True)
        i2 = jnp.min(jnp.where(rest == m2, lane, LANES), axis=-1, keepdims=True)
        ex = jnp.exp(m2 - m1)
        g1 = 1.0 / (1.0 + ex)
        comb_ref[...] = jnp.where(lane == i1, g1, jnp.where(lane == i2, ex * g1, 0.0))

    @pl.when(f == 0)
    def _():
        ce_ref[...] = jnp.sum(jnp.where(lane == e, comb_ref[...], 0.0), axis=-1, keepdims=True)

    xn = xn_ref[...]
    gate = jnp.dot(xn, wg_ref[0], preferred_element_type=F32)
    up = jnp.dot(xn, wu_ref[0], preferred_element_type=F32)
    act = (gate * _sigmoid(gate) * up * ce_ref[...]).astype(BF16)
    acc_ref[...] += jnp.dot(act, wd_ref[0], preferred_element_type=F32)

    @pl.when((e == pl.num_programs(1) - 1) & (f == pl.num_programs(2) - 1))
    def _():
        o_ref[...] = _rms(h_ref[...] + acc_ref[...], fn_ref[...])


def moe(h, g, router_pad, wg, wu, wd, final_g, *, tm=1024, tf=512, name="moe"):
    t, d = h.shape
    n_e, _, dff = wg.shape
    return pl.pallas_call(
        _moe_kernel,
        grid=(t // tm, n_e, dff // tf),
        in_specs=[pl.BlockSpec((tm, d), lambda i, e, f: (i, 0)),
                  pl.BlockSpec((1, d), lambda i, e, f: (0, 0)),
                  pl.BlockSpec((d, LANES), lambda i, e, f: (0, 0)),
                  pl.BlockSpec((1, d, tf), lambda i, e, f: (e, 0, f)),
                  pl.BlockSpec((1, d, tf), lambda i, e, f: (e, 0, f)),
                  pl.BlockSpec((1, tf, d), lambda i, e, f: (e, f, 0)),
                  pl.BlockSpec((1, d), lambda i, e, f: (0, 0))],
        out_specs=pl.BlockSpec((tm, d), lambda i, e, f: (i, 0)),
        out_shape=jax.ShapeDtypeStruct((t, d), F32),
        scratch_shapes=[pltpu.VMEM((tm, d), BF16), pltpu.VMEM((tm, d), F32),
                        pltpu.VMEM((tm, LANES), F32), pltpu.VMEM((tm, 1), F32)],
        compiler_params=_cparams("parallel", "arbitrary", "arbitrary"),
        name=name,
    )(h, g.reshape(1, d), router_pad, wg, wu, wd, final_g.reshape(1, d))


def _flash_kernel(tq, q_ref, k_ref, v_ref, o_ref, m_ref, l_ref, acc_ref):
    i = pl.program_id(2)
    row = lax.broadcasted_iota(jnp.int32, (tq, tq), 0)
    col = lax.broadcasted_iota(jnp.int32, (tq, tq), 1)
    causal = col <= row

    def step(kb, h, diagonal):
        start = pl.multiple_of(kb * tq, tq)
        q = q_ref[0, :, h * HEAD_LANES:(h + 1) * HEAD_LANES]
        k = k_ref[0, pl.ds(start, tq), h * HEAD_LANES:(h + 1) * HEAD_LANES]
        s = lax.dot_general(q, k, _NT, preferred_element_type=F32)
        if diagonal:
            s = jnp.where(causal, s, NEG_BIG)
            m_new = jnp.max(s, axis=-1, keepdims=True)
            p = jnp.exp(s - m_new)
            l_ref[h] = jnp.sum(p, axis=-1, keepdims=True)
            acc_ref[h] = jnp.dot(p.astype(BF16), v_ref[0, pl.ds(start, tq), :], preferred_element_type=F32)
        else:
            m_old = m_ref[h]
            m_new = jnp.maximum(m_old, jnp.max(s, axis=-1, keepdims=True))
            alpha = jnp.exp(m_old - m_new)
            p = jnp.exp(s - m_new)
            l_ref[h] = alpha * l_ref[h] + jnp.sum(p, axis=-1, keepdims=True)
            acc_ref[h] = alpha * acc_ref[h] + jnp.dot(p.astype(BF16), v_ref[0, pl.ds(start, tq), :],
                                                      preferred_element_type=F32)
        m_ref[h] = m_new

    step(i, 0, True)
    step(i, 1, True)

    def body(kb, carry):
        step(kb, 0, False)
        step(kb, 1, False)
        return carry

    lax.fori_loop(0, i, body, 0)

    lane = lax.broadcasted_iota(jnp.int32, (tq, 2 * HEAD_DIM), 1)
    out = jnp.where(lane < HEAD_DIM, acc_ref[0] / l_ref[0], acc_ref[1] / l_ref[1])
    o_ref[0] = out.astype(o_ref.dtype)


def flash_attention(q, k, v, *, tq=256, name="flash"):
    b, s, hw = q.shape
    n_pairs = hw // (2 * HEAD_LANES)
    return pl.pallas_call(
        functools.partial(_flash_kernel, tq),
        grid=(b, n_pairs, s // tq),
        in_specs=[pl.BlockSpec((1, tq, 2 * HEAD_LANES), lambda bi, p, i: (bi, i, p)),
                  pl.BlockSpec((1, s, 2 * HEAD_LANES), lambda bi, p, i: (bi, 0, p)),
                  pl.BlockSpec((1, s, 2 * HEAD_DIM), lambda bi, p, i: (bi, 0, p))],
        out_specs=pl.BlockSpec((1, tq, 2 * HEAD_DIM), lambda bi, p, i: (bi, i, p)),
        out_shape=jax.ShapeDtypeStruct((b, s, n_pairs * 2 * HEAD_DIM), BF16),
        scratch_shapes=[pltpu.VMEM((2, tq, 1), F32), pltpu.VMEM((2, tq, 1), F32),
                        pltpu.VMEM((2, tq, 2 * HEAD_DIM), F32)],
        compiler_params=_cparams("parallel", "parallel", "arbitrary"),
        name=name,
    )(q, k, v)


def _rwkv_kernel(prec, x_ref, mu_ref, w0_ref, w2_ref, a0_ref, a2_ref, g2_ref, kk_ref, ka_ref, rk_ref,
                 lng_ref, lnb_ref, o_ref, carry_ref, s_ref):
    c = RW_CHUNK
    gl = RW_GROUP_LANES
    n_groups = GROUP_WIDTH // gl

    @pl.when(pl.program_id(1) == 0)
    def _():
        carry_ref[...] = jnp.zeros_like(carry_ref)
        s_ref[...] = jnp.zeros_like(s_ref)

    x = x_ref[0]
    row = lax.broadcasted_iota(jnp.int32, x.shape, 0)
    prev = jnp.where(row == 0, carry_ref[0:1, :], pltpu.roll(x, 1, axis=0))
    carry_ref[0:1, :] = x[c - 1:c, :]
    xs = x + mu_ref[...] * (prev - x)

    gw = GROUP_WIDTH
    r, k, v = xs[:, 0:gw], xs[:, gw:2 * gw], xs[:, 2 * gw:3 * gw]
    wa = xs[:, 3 * gw:3 * gw + DECAY_LORA + AAA_LORA]
    gd = xs[:, 3 * gw + DECAY_LORA + AAA_LORA:]
    log_w = -_softplus(-(w0_ref[...] + _dot(jnp.tanh(wa), w2_ref[...], prec))) - 0.5
    lw = -jnp.exp(log_w)
    a = _sigmoid(a0_ref[...] + _dot(wa, a2_ref[...], prec))
    g = _dot(_sigmoid(gd), g2_ref[...], prec)

    brow = lax.broadcasted_iota(jnp.int32, (gl, gl), 0) // HEAD_DIM
    bcol = lax.broadcasted_iota(jnp.int32, (gl, gl), 1) // HEAD_DIM
    bmask = brow == bcol
    bones = jnp.where(bmask, 1.0, 0.0).astype(BF16)

    def head_sum(t):
        return jnp.concatenate([_dot(t[:, i * gl:(i + 1) * gl], bones, 2) for i in range(n_groups)], axis=1)

    kk = k * kk_ref[...]
    kk = kk / jnp.maximum(jnp.sqrt(head_sum(kk * kk)), 1e-12)
    k2 = k * (1.0 + (a - 1.0) * ka_ref[...])
    b = kk * a

    ti = lax.broadcasted_iota(jnp.int32, (c, c), 0)
    si = lax.broadcasted_iota(jnp.int32, (c, c), 1)
    log_p = _dot(jnp.where(si <= ti, 1.0, 0.0), lw, 6)
    log_pc = log_p[c - 1:c, :]
    inv_p = jnp.exp(-log_p)
    to_end = jnp.exp(log_pc - log_p)
    a_t = -kk * jnp.exp(log_p - lw)
    r_t = r * jnp.exp(log_p)
    b_t, k_t = b * inv_p, k2 * inv_p
    b_e, k_e = b * to_end, k2 * to_end
    p_c = jnp.exp(log_pc)

    tl = lax.broadcasted_iota(jnp.int32, (c, gl), 0)
    sl = lax.broadcasted_iota(jnp.int32, (c, gl), 1) % HEAD_DIM
    strict, incl = sl < tl, sl <= tl
    eye_l = jnp.where(sl == tl, 1.0, 0.0)
    er = lax.broadcasted_iota(jnp.int32, (gl, gl), 0)
    ec = lax.broadcasted_iota(jnp.int32, (gl, gl), 1)

    def bd(t):
        return jnp.where(bmask, jnp.concatenate([t] * RW_GROUP_HEADS, axis=0), 0.0)

    ys = []
    for gi in range(n_groups):
        sl_g = slice(gi * gl, (gi + 1) * gl)
        at_g, rt_g, bt_g, kt_g = a_t[:, sl_g], r_t[:, sl_g], b_t[:, sl_g], k_t[:, sl_g]
        be_g, ke_g, v_g = b_e[:, sl_g], k_e[:, sl_g], v[:, sl_g]
        gram = _dot_general(jnp.concatenate([at_g, rt_g], axis=0),
                            jnp.concatenate([bd(bt_g), bd(kt_g)], axis=0), _NT, prec)
        a_ab = jnp.where(strict, gram[:c, :gl], 0.0)
        a_ak = jnp.where(strict, gram[:c, gl:], 0.0)
        a_rb = jnp.where(incl, gram[c:, :gl], 0.0)
        a_rk = jnp.where(incl, gram[c:, gl:], 0.0)
        pw = _dot(a_ab, bd(a_ab), prec)
        tinv = eye_l + a_ab
        n_sq = int(np.log2(c)) - 1
        for it in range(n_sq):
            if it + 1 < n_sq:
                both = _dot(jnp.concatenate([tinv, pw], axis=0), bd(pw), prec)
                tinv, pw = tinv + both[:c], both[c:]
            else:
                tinv = tinv + _dot(tinv, bd(pw), prec)
        akv = _dot(jnp.concatenate([a_ak, a_rk], axis=0), bd(v_g), prec)
        wu = _dot(tinv, jnp.concatenate([bd(at_g), bd(akv[:c])], axis=1), prec)
        w_g, u0 = wu[:, :gl], wu[:, gl:]
        rwu = _dot(a_rb, jnp.concatenate([bd(w_g), bd(u0)], axis=1), prec)
        r_hat = rt_g + rwu[:, :gl]
        y0 = akv[c:] + rwu[:, gl:]
        m_bd = (jnp.where(bmask, _dot_general(be_g, w_g, _TN, prec), 0.0)
                + jnp.where(er == ec, p_c[:, sl_g], 0.0))
        z_bd = jnp.where(bmask, _dot_general(jnp.concatenate([ke_g, be_g], axis=0),
                                             jnp.concatenate([v_g, u0], axis=0), _TN, prec), 0.0)
        s0 = s_ref[gi]
        ys.append(_dot(r_hat, s0, prec) + y0)
        s_ref[gi] = _dot(m_bd, s0, prec) + z_bd
    y = jnp.concatenate(ys, axis=1)

    inv_n = 1.0 / HEAD_DIM
    d = y - head_sum(y) * inv_n
    yn = d * lax.rsqrt(head_sum(d * d) * inv_n + RW_GN_EPS) * lng_ref[...] + lnb_ref[...]
    bonus = head_sum(r * k2 * rk_ref[...]) * v
    o_ref[0] = ((yn + bonus) * g).astype(o_ref.dtype)


def rwkv7(rw, mu, w0, w2, a0, a2, g2, k_k, k_a, r_k, lnx_g, lnx_b, *, prec=3, name="rwkv7"):
    b, s, cols = rw.shape
    gw = GROUP_WIDTH
    zeros = jnp.zeros((DECAY_LORA, gw), F32)
    w2p = jnp.concatenate([w2, zeros], axis=0)
    a2p = jnp.concatenate([zeros, a2], axis=0)
    vec = lambda t: t.reshape(1, -1).astype(F32)
    params = [vec(mu), vec(w0), w2p, vec(a0), a2p, g2, vec(k_k), vec(k_a), vec(r_k), vec(lnx_g), vec(lnx_b)]
    return pl.pallas_call(
        functools.partial(_rwkv_kernel, prec),
        grid=(b, s // RW_CHUNK),
        in_specs=[pl.BlockSpec((1, RW_CHUNK, cols), lambda bi, ci: (bi, ci, 0))]
                 + [pl.BlockSpec(p.shape, lambda bi, ci: (0, 0)) for p in params],
        out_specs=pl.BlockSpec((1, RW_CHUNK, gw), lambda bi, ci: (bi, ci, 0)),
        out_shape=jax.ShapeDtypeStruct((b, s, gw), BF16),
        scratch_shapes=[pltpu.VMEM((8, cols), F32),
                        pltpu.VMEM((gw // RW_GROUP_LANES, RW_GROUP_LANES, RW_GROUP_LANES), F32)],
        compiler_params=_cparams("parallel", "arbitrary"),
        name=name,
    )(rw, *params)


def _tile_lanes(t, n):
    return jnp.concatenate([t] * n, axis=1)


def _expand_matrix():
    e = np.zeros((GROUP_WIDTH, N_HEADS * HEAD_LANES), np.float32)
    for h in range(N_HEADS):
        e[h * HEAD_DIM + np.arange(HEAD_DIM), h * HEAD_LANES + np.arange(HEAD_DIM)] = 1.0
    return jnp.asarray(e, BF16)


def _mla_prep_kernel(x_ref, qn_ref, kvn_ref, wq_ref, wqr_ref, wk_ref, wv_ref,
                     cq_ref, sq_ref, ck_ref, sk_ref, q_ref, k_ref, v_ref):
    x = x_ref[...]
    q_lat = x[:, :MLA_Q_RANK]
    kv_lat = x[:, MLA_Q_RANK:MLA_Q_RANK + MLA_KV_RANK]
    kpe = x[:, MLA_Q_RANK + MLA_KV_RANK:MLA_Q_RANK + MLA_KV_RANK + HEAD_LANES]
    kpe_rot = x[:, MLA_Q_RANK + MLA_KV_RANK + HEAD_LANES:]
    qn = _rms(q_lat, qn_ref[...]).astype(BF16)
    q = (jnp.dot(qn, wq_ref[...], preferred_element_type=F32) * _tile_lanes(cq_ref[...], N_HEADS)
         + jnp.dot(qn, wqr_ref[...], preferred_element_type=F32) * _tile_lanes(sq_ref[...], N_HEADS))
    q_ref[...] = q.astype(BF16)
    kn = _rms(kv_lat, kvn_ref[...]).astype(BF16)
    k_rope = kpe * ck_ref[...] + kpe_rot * sk_ref[...]
    k = jnp.dot(kn, wk_ref[...], preferred_element_type=F32) + _tile_lanes(k_rope, N_HEADS)
    k_ref[...] = k.astype(BF16)
    v_ref[...] = jnp.dot(kn, wv_ref[...], preferred_element_type=F32).astype(BF16)


def mla_prep(x, q_norm, kv_norm, w_uq, w_ukv, seq_len, *, tm=256, name="mla_prep"):
    t = x.shape[0]
    nope, rope, half = HEAD_DIM, MLA_ROPE, MLA_ROPE // 2
    wq = w_uq.reshape(MLA_Q_RANK, N_HEADS, nope + rope)
    wq_pe = wq[:, :, nope:]
    pad = lambda a: jnp.pad(a, ((0, 0), (0, 0), (0, HEAD_LANES - a.shape[2])))
    flat = lambda a: a.reshape(a.shape[0], -1).astype(BF16)
    wq_aug = flat(pad(wq))
    wq_rot = flat(pad(jnp.concatenate([jnp.zeros_like(wq[:, :, :nope]), -wq_pe[:, :, half:], wq_pe[:, :, :half]], axis=2)))
    wkv = w_ukv.reshape(MLA_KV_RANK, N_HEADS, 2 * HEAD_DIM)
    wk_aug = flat(pad(wkv[:, :, :nope]))
    wv = flat(wkv[:, :, nope:])

    scale = float(nope + rope) ** -0.5
    inv_freq = ROPE_THETA ** (-jnp.arange(0, rope, 2, dtype=F32) / rope)
    ang = jnp.arange(seq_len, dtype=F32)[:, None] * inv_freq[None, :]
    cos2, sin2 = _tile_lanes(jnp.cos(ang), 2), _tile_lanes(jnp.sin(ang), 2)
    z64, z32 = jnp.zeros((seq_len, nope), F32), jnp.zeros((seq_len, HEAD_LANES - nope - rope), F32)
    cq = jnp.concatenate([jnp.full((seq_len, nope), scale, F32), cos2 * scale, z32], axis=1)
    sq = jnp.concatenate([z64, sin2 * scale, z32], axis=1)
    ck = jnp.concatenate([z64, cos2, z32], axis=1)
    sk = jnp.concatenate([z64, sin2, z32], axis=1)

    n_seq_tiles = seq_len // tm
    tab_spec = pl.BlockSpec((tm, HEAD_LANES), lambda i: (i % n_seq_tiles, 0))
    weights = [wq_aug, wq_rot, wk_aug, wv]
    return pl.pallas_call(
        _mla_prep_kernel,
        grid=(t // tm,),
        in_specs=[_row_spec(tm, x.shape[1]), _full_spec((1, MLA_Q_RANK)), _full_spec((1, MLA_KV_RANK))]
                 + [_full_spec(w.shape) for w in weights] + [tab_spec] * 4,
        out_specs=[_row_spec(tm, N_HEADS * HEAD_LANES), _row_spec(tm, N_HEADS * HEAD_LANES),
                   _row_spec(tm, GROUP_WIDTH)],
        out_shape=[jax.ShapeDtypeStruct((t, N_HEADS * HEAD_LANES), BF16),
                   jax.ShapeDtypeStruct((t, N_HEADS * HEAD_LANES), BF16),
                   jax.ShapeDtypeStruct((t, GROUP_WIDTH), BF16)],
        compiler_params=_cparams("parallel"),
        name=name,
    )(x, q_norm.reshape(1, -1), kv_norm.reshape(1, -1), *weights, cq, sq, ck, sk)


FOX_PARTS = 3


def _fox_prep_kernel(scale, q_ref, k_ref, f_ref, bf_ref, e_ref, pq_ref, pk_ref, cq_ref, ck_ref,
                     qo_ref, ko_ref, carry_ref):
    @pl.when(pl.program_id(1) == 0)
    def _():
        carry_ref[...] = jnp.zeros_like(carry_ref)

    tm = q_ref.shape[1]
    log_f = -_softplus(-(f_ref[0] + bf_ref[...]))
    ti = lax.broadcasted_iota(jnp.int32, (tm, tm), 0)
    si = lax.broadcasted_iota(jnp.int32, (tm, tm), 1)
    d = _dot(jnp.where(si <= ti, 1.0, 0.0), log_f, 6) + carry_ref[0:1, :]
    carry_ref[0:1, :] = d[tm - 1:tm, :]
    hi = d.astype(BF16)
    rem = d - hi.astype(F32)
    mid = rem.astype(BF16)
    lo = (rem - mid.astype(F32)).astype(BF16)
    parts = jnp.concatenate([hi, mid, lo], axis=1)
    q = (jnp.dot((q_ref[0] * scale).astype(BF16), e_ref[...], preferred_element_type=F32)
         + jnp.dot(parts, pq_ref[...], preferred_element_type=F32) + cq_ref[...])
    k = (jnp.dot(k_ref[0].astype(BF16), e_ref[...], preferred_element_type=F32)
         + jnp.dot(parts, pk_ref[...], preferred_element_type=F32) + ck_ref[...])
    qo_ref[0] = q.astype(BF16)
    ko_ref[0] = k.astype(BF16)


def fox_prep(q, k, f_logit, b_f, *, tm=256, name="fox_prep"):
    b, s, _ = q.shape
    width = N_HEADS * HEAD_LANES
    pq = np.zeros((FOX_PARTS * LANES, width), np.float32)
    pk = np.zeros((FOX_PARTS * LANES, width), np.float32)
    cq = np.zeros((1, width), np.float32)
    ck = np.zeros((1, width), np.float32)
    for h in range(N_HEADS):
        base = h * HEAD_LANES + HEAD_DIM
        for p in range(FOX_PARTS):
            pk[p * LANES + h, base + p] = -1.0
            pq[p * LANES + h, base + FOX_PARTS + p] = 1.0
            cq[0, base + p] = 1.0
            ck[0, base + FOX_PARTS + p] = 1.0
    consts = [_expand_matrix(), jnp.asarray(pq, BF16), jnp.asarray(pk, BF16), jnp.asarray(cq), jnp.asarray(ck)]
    bf_pad = jnp.zeros((1, LANES), F32).at[0, :N_HEADS].set(b_f.astype(F32))
    blk = lambda n: pl.BlockSpec((1, tm, n), lambda bi, i: (bi, i, 0))
    return pl.pallas_call(
        functools.partial(_fox_prep_kernel, float(HEAD_DIM) ** -0.5),
        grid=(b, s // tm),
        in_specs=[blk(GROUP_WIDTH), blk(GROUP_WIDTH), blk(LANES), pl.BlockSpec((1, LANES), lambda bi, i: (0, 0))]
                 + [pl.BlockSpec(c.shape, lambda bi, i: (0, 0)) for c in consts],
        out_specs=[blk(width), blk(width)],
        out_shape=[jax.ShapeDtypeStruct((b, s, width), BF16)] * 2,
        scratch_shapes=[pltpu.VMEM((8, LANES), F32)],
        compiler_params=_cparams("parallel", "arbitrary"),
        name=name,
    )(q, k, f_logit, bf_pad, *consts)


MOBA_NB_PAD = 32


def _rope_full(x, cos, sin):
    half = HEAD_DIM // 2
    lane = lax.broadcasted_iota(jnp.int32, x.shape, 1)
    width = x.shape[1]
    rot = jnp.where(lane % HEAD_DIM < half, -pltpu.roll(x, width - half, axis=1), pltpu.roll(x, half, axis=1))
    return x * cos + rot * sin


def _moba_prep_kernel(q_ref, k_ref, cos_ref, sin_ref, e_ref, qo_ref, ko_ref, km_ref):
    blk = pl.program_id(1)
    cos = _tile_lanes(cos_ref[...], GROUP_WIDTH // LANES)
    sin = _tile_lanes(sin_ref[...], GROUP_WIDTH // LANES)
    qo_ref[0] = _rope_full(q_ref[0], cos, sin)
    k = _rope_full(k_ref[0], cos, sin)
    km_ref[0, 0] = jnp.mean(k, axis=0, keepdims=True)
    lane = lax.broadcasted_iota(jnp.int32, ko_ref.shape[1:], 1) % HEAD_LANES
    own_lane = jnp.where(lane == HEAD_DIM + blk, 1.0, 0.0)
    ko_ref[0] = (jnp.dot(k.astype(BF16), e_ref[...], preferred_element_type=F32) + own_lane).astype(BF16)


def moba_prep(q, k, *, name="moba_prep"):
    b, s, _ = q.shape
    tm = MOBA_BLOCK
    nb = s // tm
    assert nb <= MOBA_NB_PAD
    inv_freq = ROPE_THETA ** (-jnp.arange(0, HEAD_DIM, 2, dtype=F32) / HEAD_DIM)
    ang = jnp.arange(s, dtype=F32)[:, None] * inv_freq[None, :]
    cos, sin = _tile_lanes(jnp.cos(ang), LANES // (HEAD_DIM // 2)), _tile_lanes(jnp.sin(ang), LANES // (HEAD_DIM // 2))
    width = N_HEADS * HEAD_LANES
    blk = lambda n: pl.BlockSpec((1, tm, n), lambda bi, i: (bi, i, 0))
    tab = pl.BlockSpec((tm, LANES), lambda bi, i: (i, 0))
    e = _expand_matrix()
    return pl.pallas_call(
        _moba_prep_kernel,
        grid=(b, nb),
        in_specs=[blk(GROUP_WIDTH), blk(GROUP_WIDTH), tab, tab, pl.BlockSpec(e.shape, lambda bi, i: (0, 0))],
        out_specs=[blk(GROUP_WIDTH), blk(width),
                   pl.BlockSpec((1, 1, 1, GROUP_WIDTH), lambda bi, i: (bi, i, 0, 0))],
        out_shape=[jax.ShapeDtypeStruct((b, s, GROUP_WIDTH), F32), jax.ShapeDtypeStruct((b, s, width), BF16),
                   jax.ShapeDtypeStruct((b, nb, 1, GROUP_WIDTH), F32)],
        compiler_params=_cparams("parallel", "parallel"),
        name=name,
    )(q, k, cos, sin, e)


def _moba_gate_kernel(scale, q_ref, km_ref, e_ref, pm_ref, qo_ref):
    own = pl.program_id(1)
    q = q_ref[0]
    gate_t = _dot_general(km_ref[0], q, _NT, 6)
    nbp = MOBA_NB_PAD
    j = lax.broadcasted_iota(jnp.int32, (nbp, q.shape[0]), 0)
    valid = j < own
    lowest = -3.0e38
    bias_t = []
    for h in range(N_HEADS):
        g = jnp.where(valid, gate_t[h * nbp:(h + 1) * nbp, :], lowest)
        rest = g
        for _ in range(MOBA_TOPK - 1):
            rest = jnp.where(rest >= jnp.max(rest, axis=0, keepdims=True), lowest, rest)
        kth = jnp.max(rest, axis=0, keepdims=True)
        keep = (valid & (g >= kth)) | (j == own)
        bias_t.append(jnp.where(keep, 0.0, NEG_BIG))
    bias = jnp.concatenate(bias_t, axis=0).T
    out = (jnp.dot((q * scale).astype(BF16), e_ref[...], preferred_element_type=F32)
           + jnp.dot(bias.astype(BF16), pm_ref[...], preferred_element_type=F32))
    qo_ref[0] = out.astype(BF16)


def moba_gate(q_rope, k_mean, *, name="moba_gate"):
    b, s, _ = q_rope.shape
    tm = MOBA_BLOCK
    nb = s // tm
    nbp = MOBA_NB_PAD
    width = N_HEADS * HEAD_LANES
    km = k_mean.reshape(b, nb, N_HEADS, HEAD_DIM).transpose(0, 2, 1, 3)
    km = jnp.pad(km, ((0, 0), (0, 0), (0, nbp - nb), (0, 0)))
    eye = jnp.eye(N_HEADS, dtype=F32)
    km_bd = (km[:, :, :, None, :] * eye[None, :, None, :, None]).reshape(b, N_HEADS * nbp, GROUP_WIDTH)
    pm = np.zeros((N_HEADS * nbp, width), np.float32)
    for h in range(N_HEADS):
        pm[h * nbp + np.arange(nbp), h * HEAD_LANES + HEAD_DIM + np.arange(nbp)] = 1.0
    e, pm = _expand_matrix(), jnp.asarray(pm, BF16)
    return pl.pallas_call(
        functools.partial(_moba_gate_kernel, float(HEAD_DIM) ** -0.5),
        grid=(b, nb),
        in_specs=[pl.BlockSpec((1, tm, GROUP_WIDTH), lambda bi, i: (bi, i, 0)),
                  pl.BlockSpec((1, N_HEADS * nbp, GROUP_WIDTH), lambda bi, i: (bi, 0, 0)),
                  pl.BlockSpec(e.shape, lambda bi, i: (0, 0)), pl.BlockSpec(pm.shape, lambda bi, i: (0, 0))],
        out_specs=pl.BlockSpec((1, tm, width), lambda bi, i: (bi, i, 0)),
        out_shape=jax.ShapeDtypeStruct((b, s, width), BF16),
        compiler_params=_cparams("parallel", "parallel"),
        name=name,
    )(q_rope, km_bd, e, pm)


def kernel(x, norm_mix_0, w_in_0, shift_mu_0, rw_w0_0, rw_w2_0, rw_a0_0, rw_a2_0, rw_g2_0, rw_kk_0, rw_ka_0,
           rw_rk_0, rw_lnx_g_0, rw_lnx_b_0, mla_qnorm_0, mla_wuq_0, mla_kvnorm_0, mla_wukv_0, w_out_0,
           norm_ffn_0, ffn_wg_0, ffn_wu_0, ffn_wd_0, norm_mix_1, w_in_1, fox_bf_1, w_out_1, norm_ffn_1,
           router_1, moe_wg_1, moe_wu_1, moe_wd_1, final_norm):
    b, s, d = x.shape
    t = b * s
    gw = GROUP_WIDTH
    bf = lambda w: w.astype(BF16)
    x2 = x.reshape(t, d)

    mla0 = RW_COLS
    w_q, w_kv = w_in_0[:, mla0:mla0 + MLA_Q_RANK], w_in_0[:, mla0 + MLA_Q_RANK:mla0 + MLA_Q_RANK + MLA_KV_RANK]
    w_kr = w_in_0[:, mla0 + MLA_Q_RANK + MLA_KV_RANK:]
    half = MLA_ROPE // 2
    z = lambda n: jnp.zeros((d, n), F32)
    w_kpe = jnp.concatenate([z(HEAD_DIM), w_kr, z(HEAD_LANES - HEAD_DIM - MLA_ROPE)], axis=1)
    w_kpe_rot = jnp.concatenate([z(HEAD_DIM), -w_kr[:, half:], w_kr[:, :half],
                                 z(HEAD_LANES - HEAD_DIM - MLA_ROPE)], axis=1)
    w_mla = jnp.concatenate([w_q, w_kv, w_kpe, w_kpe_rot], axis=1)
    rw, mla = norm_proj(x2, norm_mix_0, [bf(w_in_0[:, :RW_COLS]), bf(w_mla)], [F32, F32], name="in_proj_0")
    y_a = rwkv7(rw.reshape(b, s, RW_COLS), shift_mu_0, rw_w0_0, rw_w2_0, rw_a0_0, rw_a2_0, rw_g2_0,
                rw_kk_0, rw_ka_0, rw_rk_0, rw_lnx_g_0, rw_lnx_b_0)
    q, k, v = mla_prep(mla, mla_qnorm_0, mla_kvnorm_0, mla_wuq_0, mla_wukv_0, s)
    y_b = flash_attention(q.reshape(b, s, -1), k.reshape(b, s, -1), v.reshape(b, s, -1), name="flash_mla")
    h = out_proj(y_a.reshape(t, gw), y_b.reshape(t, gw), bf(w_out_0[:gw]), bf(w_out_0[gw:]), x2, name="out_proj_0")
    h = ffn(h, norm_ffn_0, bf(ffn_wg_0), bf(ffn_wu_0), bf(ffn_wd_0))

    c0 = 3 * gw
    w_f = jnp.pad(w_in_1[:, c0:c0 + N_HEADS], ((0, 0), (0, LANES - N_HEADS)))
    c1 = c0 + N_HEADS
    cols = [w_in_1[:, 0:gw], w_in_1[:, gw:2 * gw], w_in_1[:, 2 * gw:3 * gw], w_f,
            w_in_1[:, c1:c1 + gw], w_in_1[:, c1 + gw:c1 + 2 * gw], w_in_1[:, c1 + 2 * gw:]]
    fq, fk, fv, ff, mq, mk, mv = norm_proj(h, norm_mix_1, [bf(w) for w in cols],
                                           [F32, F32, BF16, F32, F32, F32, BF16], name="in_proj_1")
    r3 = lambda a: a.reshape(b, s, -1)
    fqa, fka = fox_prep(r3(fq), r3(fk), r3(ff), fox_bf_1)
    y_c = flash_attention(fqa, fka, r3(fv), name="flash_fox")
    mq_rope, mka, k_mean = moba_prep(r3(mq), r3(mk))
    mqa = moba_gate(mq_rope, k_mean)
    y_d = flash_attention(mqa, mka, r3(mv), name="flash_moba")
    h = out_proj(y_c.reshape(t, gw), y_d.reshape(t, gw), bf(w_out_1[:gw]), bf(w_out_1[gw:]), h, name="out_proj_1")

    router_pad = jnp.pad(router_1, ((0, 0), (0, LANES - N_EXPERTS)))
    out = moe(h, norm_ffn_1, router_pad, bf(moe_wg_1), bf(moe_wu_1), bf(moe_wd_1), final_norm)
    return out.reshape(b, s, d)
```

```python
import functools

import numpy as np
import jax
import jax.numpy as jnp
from jax import lax
from jax.experimental import pallas as pl
from jax.experimental.pallas import tpu as pltpu

F32 = jnp.float32
BF16 = jnp.bfloat16

HEAD_DIM = 64
N_HEADS = 8
GROUP_WIDTH = N_HEADS * HEAD_DIM
HEAD_LANES = 128
MLA_ROPE = 32
MLA_Q_RANK = 256
MLA_KV_RANK = 128
DECAY_LORA = 64
AAA_LORA = 64
GATE_LORA = 128
RW_COLS = 3 * GROUP_WIDTH + DECAY_LORA + AAA_LORA + GATE_LORA
RW_GN_EPS = 64e-5
MOBA_BLOCK = 256
MOBA_TOPK = 3
ROPE_THETA = 10000.0
NORM_EPS = 1e-6
N_EXPERTS = 8
NEG_BIG = -1e30

LANES = 128
VMEM_LIMIT_BYTES = 56 * 1024 * 1024

RW_CHUNK = 64
RW_GROUP_HEADS = 4
RW_GROUP_LANES = RW_GROUP_HEADS * HEAD_DIM


def _cparams(*semantics):
    return pltpu.CompilerParams(dimension_semantics=semantics, vmem_limit_bytes=VMEM_LIMIT_BYTES)


def _dot(a, b, prec=1):
    if prec == 6:
        return jnp.dot(a.astype(F32), b.astype(F32), preferred_element_type=F32,
                       precision=lax.Precision.HIGHEST)
    if prec == 1:
        return jnp.dot(a.astype(BF16), b.astype(BF16), preferred_element_type=F32)
    ah, al = _split(a)
    bh, bl = _split(b)
    return (jnp.dot(ah, bh, preferred_element_type=F32) + jnp.dot(ah, bl, preferred_element_type=F32)
            + jnp.dot(al, bh, preferred_element_type=F32))


def _split(x):
    if x.dtype == BF16:
        return x, jnp.zeros_like(x)
    hi = x.astype(BF16)
    return hi, (x - hi.astype(F32)).astype(BF16)


_NT = (((1,), (1,)), ((), ()))
_TN = (((0,), (0,)), ((), ()))


def _dot_general(a, b, dims, prec=1):
    if prec == 6:
        return lax.dot_general(a.astype(F32), b.astype(F32), dims, preferred_element_type=F32,
                               precision=lax.Precision.HIGHEST)
    if prec == 1:
        return lax.dot_general(a.astype(BF16), b.astype(BF16), dims, preferred_element_type=F32)
    ah, al = _split(a)
    bh, bl = _split(b)
    return (lax.dot_general(ah, bh, dims, preferred_element_type=F32)
            + lax.dot_general(ah, bl, dims, preferred_element_type=F32)
            + lax.dot_general(al, bh, dims, preferred_element_type=F32))


def _rms(x, g):
    return x * lax.rsqrt(jnp.mean(x * x, axis=-1, keepdims=True) + NORM_EPS) * g


def _softplus(x):
    return jnp.maximum(x, 0.0) + jnp.log(1.0 + jnp.exp(-jnp.abs(x)))


def _sigmoid(x):
    return 1.0 / (1.0 + jnp.exp(-x))


def _row_spec(tm, n):
    return pl.BlockSpec((tm, n), lambda i: (i, 0))


def _full_spec(shape):
    nd = len(shape)
    return pl.BlockSpec(shape, lambda *_: (0,) * nd)


def _norm_proj_kernel(n_out, x_ref, g_ref, *refs):
    w_refs, o_refs = refs[:n_out], refs[n_out:]
    xn = _rms(x_ref[...], g_ref[...]).astype(BF16)
    for w_ref, o_ref in zip(w_refs, o_refs):
        o_ref[...] = jnp.dot(xn, w_ref[...], preferred_element_type=F32).astype(o_ref.dtype)


def norm_proj(x, g, weights, dtypes, *, tm=256, name="norm_proj"):
    t, d = x.shape
    n_out = len(weights)
    return pl.pallas_call(
        functools.partial(_norm_proj_kernel, n_out),
        grid=(t // tm,),
        in_specs=[_row_spec(tm, d), _full_spec((1, d))] + [_full_spec(w.shape) for w in weights],
        out_specs=[_row_spec(tm, w.shape[1]) for w in weights],
        out_shape=[jax.ShapeDtypeStruct((t, w.shape[1]), dt) for w, dt in zip(weights, dtypes)],
        compiler_params=_cparams("parallel"),
        name=name,
    )(x, g.reshape(1, d), *weights)


def _out_proj_kernel(y1_ref, y2_ref, w1_ref, w2_ref, r_ref, o_ref):
    o_ref[...] = (r_ref[...] + jnp.dot(y1_ref[...], w1_ref[...], preferred_element_type=F32)
                  + jnp.dot(y2_ref[...], w2_ref[...], preferred_element_type=F32))


def out_proj(y1, y2, w1, w2, resid, *, tm=512, name="out_proj"):
    t, d = resid.shape
    return pl.pallas_call(
        _out_proj_kernel,
        grid=(t // tm,),
        in_specs=[_row_spec(tm, y1.shape[1]), _row_spec(tm, y2.shape[1]),
                  _full_spec(w1.shape), _full_spec(w2.shape), _row_spec(tm, d)],
        out_specs=_row_spec(tm, d),
        out_shape=jax.ShapeDtypeStruct((t, d), F32),
        compiler_params=_cparams("parallel"),
        name=name,
    )(y1, y2, w1, w2, resid)


def _ffn_kernel(h_ref, g_ref, wg_ref, wu_ref, wd_ref, o_ref, xn_ref, acc_ref):
    f = pl.program_id(1)

    @pl.when(f == 0)
    def _():
        xn_ref[...] = _rms(h_ref[...], g_ref[...]).astype(BF16)
        acc_ref[...] = jnp.zeros_like(acc_ref)

    xn = xn_ref[...]
    gate = jnp.dot(xn, wg_ref[...], preferred_element_type=F32)
    up = jnp.dot(xn, wu_ref[...], preferred_element_type=F32)
    act = (gate * _sigmoid(gate) * up).astype(BF16)
    acc_ref[...] += jnp.dot(act, wd_ref[...], preferred_element_type=F32)

    @pl.when(f == pl.num_programs(1) - 1)
    def _():
        o_ref[...] = h_ref[...] + acc_ref[...]


def ffn(h, g, wg, wu, wd, *, tm=1024, tf=256, name="ffn"):
    t, d = h.shape
    dff = wg.shape[1]
    return pl.pallas_call(
        _ffn_kernel,
        grid=(t // tm, dff // tf),
        in_specs=[pl.BlockSpec((tm, d), lambda i, f: (i, 0)),
                  pl.BlockSpec((1, d), lambda i, f: (0, 0)),
                  pl.BlockSpec((d, tf), lambda i, f: (0, f)),
                  pl.BlockSpec((d, tf), lambda i, f: (0, f)),
                  pl.BlockSpec((tf, d), lambda i, f: (f, 0))],
        out_specs=pl.BlockSpec((tm, d), lambda i, f: (i, 0)),
        out_shape=jax.ShapeDtypeStruct((t, d), F32),
        scratch_shapes=[pltpu.VMEM((tm, d), BF16), pltpu.VMEM((tm, d), F32)],
        compiler_params=_cparams("parallel", "arbitrary"),
        name=name,
    )(h, g.reshape(1, d), wg, wu, wd)


def _moe_kernel(h_ref, g_ref, router_ref, wg_ref, wu_ref, wd_ref, fn_ref, o_ref,
                xn_ref, acc_ref, comb_ref, ce_ref):
    e = pl.program_id(1)
    f = pl.program_id(2)
    lane = lax.broadcasted_iota(jnp.int32, comb_ref.shape, 1)

    @pl.when((e == 0) & (f == 0))
    def _():
        t = _rms(h_ref[...], g_ref[...])
        xn_ref[...] = t.astype(BF16)
        acc_ref[...] = jnp.zeros_like(acc_ref)
        logits = _dot(t, router_ref[...], prec=6)
        logits = jnp.where(lane < N_EXPERTS, logits, NEG_BIG)
        m1 = jnp.max(logits, axis=-1, keepdims=True)
        i1 = jnp.min(jnp.where(logits == m1, lane, LANES), axis=-1, keepdims=True)
        rest = jnp.where(lane == i1, NEG_BIG, logits)
        m2 = jnp.max(rest, axis=-1, keepdims=True)
        i2 = jnp.min(jnp.where(rest == m2, lane, LANES), axis=-1, keepdims=True)
        ex = jnp.exp(m2 - m1)
        g1 = 1.0 / (1.0 + ex)
        comb_ref[...] = jnp.where(lane == i1, g1, jnp.where(lane == i2, ex * g1, 0.0))

    @pl.when(f == 0)
    def _():
        ce_ref[...] = jnp.sum(jnp.where(lane == e, comb_ref[...], 0.0), axis=-1, keepdims=True)

    xn = xn_ref[...]
    gate = jnp.dot(xn, wg_ref[0], preferred_element_type=F32)
    up = jnp.dot(xn, wu_ref[0], preferred_element_type=F32)
    act = (gate * _sigmoid(gate) * up * ce_ref[...]).astype(BF16)
    acc_ref[...] += jnp.dot(act, wd_ref[0], preferred_element_type=F32)

    @pl.when((e == pl.num_programs(1) - 1) & (f == pl.num_programs(2) - 1))
    def _():
        o_ref[...] = _rms(h_ref[...] + acc_ref[...], fn_ref[...])


def moe(h, g, router_pad, wg, wu, wd, final_g, *, tm=1024, tf=512, name="moe"):
    t, d = h.shape
    n_e, _, dff = wg.shape
    return pl.pallas_call(
        _moe_kernel,
        grid=(t // tm, n_e, dff // tf),
        in_specs=[pl.BlockSpec((tm, d), lambda i, e, f: (i, 0)),
                  pl.BlockSpec((1, d), lambda i, e, f: (0, 0)),
                  pl.BlockSpec((d, LANES), lambda i, e, f: (0, 0)),
                  pl.BlockSpec((1, d, tf), lambda i, e, f: (e, 0, f)),
                  pl.BlockSpec((1, d, tf), lambda i, e, f: (e, 0, f)),
                  pl.BlockSpec((1, tf, d), lambda i, e, f: (e, f, 0)),
                  pl.BlockSpec((1, d), lambda i, e, f: (0, 0))],
        out_specs=pl.BlockSpec((tm, d), lambda i, e, f: (i, 0)),
        out_shape=jax.ShapeDtypeStruct((t, d), F32),
        scratch_shapes=[pltpu.VMEM((tm, d), BF16), pltpu.VMEM((tm, d), F32),
                        pltpu.VMEM((tm, LANES), F32), pltpu.VMEM((tm, 1), F32)],
        compiler_params=_cparams("parallel", "arbitrary", "arbitrary"),
        name=name,
    )(h, g.reshape(1, d), router_pad, wg, wu, wd, final_g.reshape(1, d))


LOG2E = 1.4426950408889634


def _flash_kernel(blk, q_ref, k_ref, vt_ref, o_ref, acc_ref):
    i = pl.program_id(2)
    key = lax.broadcasted_iota(jnp.int32, (blk, blk), 0)
    qry = lax.broadcasted_iota(jnp.int32, (blk, blk), 1)
    causal = key <= qry

    def scores(h, kb):
        start = pl.multiple_of(kb * blk, blk)
        k = k_ref[0, pl.ds(start, blk), h * HEAD_LANES:(h + 1) * HEAD_LANES]
        q = q_ref[0, :, h * HEAD_LANES:(h + 1) * HEAD_LANES]
        return lax.dot_general(k, q, _NT, preferred_element_type=F32)

    stats = []
    for h in range(2):
        s = jnp.where(causal, scores(h, i), NEG_BIG)
        m = jnp.max(s, axis=0, keepdims=True)
        p = jnp.exp2(s - m)
        stats += [m, jnp.sum(p, axis=0, keepdims=True)]
        acc_ref[h] = jnp.dot(vt_ref[0, 0, i], p.astype(BF16), preferred_element_type=F32)

    def body(kb, carry):
        new = []
        for h in range(2):
            m_old, l_old = carry[2 * h], carry[2 * h + 1]
            s = scores(h, kb)
            m_new = jnp.maximum(m_old, jnp.max(s, axis=0, keepdims=True))
            alpha = jnp.exp2(m_old - m_new)
            p = jnp.exp2(s - m_new)
            new += [m_new, alpha * l_old + jnp.sum(p, axis=0, keepdims=True)]
            acc_ref[h] = alpha * acc_ref[h] + jnp.dot(vt_ref[0, 0, kb], p.astype(BF16),
                                                      preferred_element_type=F32)
        return tuple(new)

    _, l0, _, l1 = lax.fori_loop(0, i, body, tuple(stats))
    vrow = lax.broadcasted_iota(jnp.int32, (2 * HEAD_DIM, blk), 0)
    out_t = jnp.where(vrow < HEAD_DIM, acc_ref[0] / l0, acc_ref[1] / l1)
    o_ref[0] = out_t.T.astype(o_ref.dtype)


def flash_attention(q, k, v, *, blk=512, name="flash"):
    b, s, hw = q.shape
    n_pairs = hw // (2 * HEAD_LANES)
    nb = s // blk
    pair = 2 * HEAD_DIM
    vt = v.reshape(b, nb, blk, n_pairs, pair).transpose(0, 3, 1, 4, 2)
    return pl.pallas_call(
        functools.partial(_flash_kernel, blk),
        grid=(b, n_pairs, nb),
        in_specs=[pl.BlockSpec((1, blk, 2 * HEAD_LANES), lambda bi, p, i: (bi, i, p)),
                  pl.BlockSpec((1, s, 2 * HEAD_LANES), lambda bi, p, i: (bi, 0, p)),
                  pl.BlockSpec((1, 1, nb, pair, blk), lambda bi, p, i: (bi, p, 0, 0, 0))],
        out_specs=pl.BlockSpec((1, blk, pair), lambda bi, p, i: (bi, i, p)),
        out_shape=jax.ShapeDtypeStruct((b, s, n_pairs * pair), BF16),
        scratch_shapes=[pltpu.VMEM((2, pair, blk), F32)],
        compiler_params=_cparams("parallel", "parallel", "arbitrary"),
        name=name,
    )(q, k, vt)


def _rwkv_kernel(prec, x_ref, mu_ref, w0_ref, w2_ref, a0_ref, a2_ref, g2_ref, kk_ref, ka_ref, rk_ref,
                 lng_ref, lnb_ref, o_ref, carry_ref, s_ref):
    c = RW_CHUNK
    gl = RW_GROUP_LANES
    n_groups = GROUP_WIDTH // gl

    @pl.when(pl.program_id(1) == 0)
    def _():
        carry_ref[...] = jnp.zeros_like(carry_ref)
        s_ref[...] = jnp.zeros_like(s_ref)

    x = x_ref[0]
    row = lax.broadcasted_iota(jnp.int32, x.shape, 0)
    prev = jnp.where(row == 0, carry_ref[0:1, :], pltpu.roll(x, 1, axis=0))
    carry_ref[0:1, :] = x[c - 1:c, :]
    xs = x + mu_ref[...] * (prev - x)

    gw = GROUP_WIDTH
    r, k, v = xs[:, 0:gw], xs[:, gw:2 * gw], xs[:, 2 * gw:3 * gw]
    wa = xs[:, 3 * gw:3 * gw + DECAY_LORA + AAA_LORA]
    gd = xs[:, 3 * gw + DECAY_LORA + AAA_LORA:]
    log_w = -_softplus(-(w0_ref[...] + _dot(jnp.tanh(wa), w2_ref[...], prec))) - 0.5
    lw = -jnp.exp(log_w)
    a = _sigmoid(a0_ref[...] + _dot(wa, a2_ref[...], prec))
    g = _dot(_sigmoid(gd), g2_ref[...], prec)

    brow = lax.broadcasted_iota(jnp.int32, (gl, gl), 0) // HEAD_DIM
    bcol = lax.broadcasted_iota(jnp.int32, (gl, gl), 1) // HEAD_DIM
    bmask = brow == bcol
    bones = jnp.where(bmask, 1.0, 0.0).astype(BF16)

    def head_sum(t):
        return jnp.concatenate([_dot(t[:, i * gl:(i + 1) * gl], bones, 2) for i in range(n_groups)], axis=1)

    kk = k * kk_ref[...]
    kk = kk / jnp.maximum(jnp.sqrt(head_sum(kk * kk)), 1e-12)
    k2 = k * (1.0 + (a - 1.0) * ka_ref[...])
    b = kk * a

    ti = lax.broadcasted_iota(jnp.int32, (c, c), 0)
    si = lax.broadcasted_iota(jnp.int32, (c, c), 1)
    log_p = _dot(jnp.where(si <= ti, 1.0, 0.0), lw, 6)
    log_pc = log_p[c - 1:c, :]
    inv_p = jnp.exp(-log_p)
    to_end = jnp.exp(log_pc - log_p)
    a_t = -kk * jnp.exp(log_p - lw)
    r_t = r * jnp.exp(log_p)
    b_t, k_t = b * inv_p, k2 * inv_p
    b_e, k_e = b * to_end, k2 * to_end
    p_c = jnp.exp(log_pc)

    tl = lax.broadcasted_iota(jnp.int32, (c, gl), 0)
    sl = lax.broadcasted_iota(jnp.int32, (c, gl), 1) % HEAD_DIM
    strict, incl = sl < tl, sl <= tl
    eye_l = jnp.where(sl == tl, 1.0, 0.0)
    er = lax.broadcasted_iota(jnp.int32, (gl, gl), 0)
    ec = lax.broadcasted_iota(jnp.int32, (gl, gl), 1)

    def bd(t):
        return jnp.where(bmask, jnp.concatenate([t] * RW_GROUP_HEADS, axis=0), 0.0)

    ys = []
    for gi in range(n_groups):
        sl_g = slice(gi * gl, (gi + 1) * gl)
        at_g, rt_g, bt_g, kt_g = a_t[:, sl_g], r_t[:, sl_g], b_t[:, sl_g], k_t[:, sl_g]
        be_g, ke_g, v_g = b_e[:, sl_g], k_e[:, sl_g], v[:, sl_g]
        gram = _dot_general(jnp.concatenate([at_g, rt_g], axis=0),
                            jnp.concatenate([bd(bt_g), bd(kt_g)], axis=0), _NT, prec)
        a_ab = jnp.where(strict, gram[:c, :gl], 0.0)
        a_ak = jnp.where(strict, gram[:c, gl:], 0.0)
        a_rb = jnp.where(incl, gram[c:, :gl], 0.0)
        a_rk = jnp.where(incl, gram[c:, gl:], 0.0)
        pw = _dot(a_ab, bd(a_ab), prec)
        tinv = eye_l + a_ab
        n_sq = int(np.log2(c)) - 1
        for it in range(n_sq):
            if it + 1 < n_sq:
                both = _dot(jnp.concatenate([tinv, pw], axis=0), bd(pw), prec)
                tinv, pw = tinv + both[:c], both[c:]
            else:
                tinv = tinv + _dot(tinv, bd(pw), prec)
        akv = _dot(jnp.concatenate([a_ak, a_rk], axis=0), bd(v_g), prec)
        wu = _dot(tinv, jnp.concatenate([bd(at_g), bd(akv[:c])], axis=1), prec)
        w_g, u0 = wu[:, :gl], wu[:, gl:]
        rwu = _dot(a_rb, jnp.concatenate([bd(w_g), bd(u0)], axis=1), prec)
        r_hat = rt_g + rwu[:, :gl]
        y0 = akv[c:] + rwu[:, gl:]
        m_bd = (jnp.where(bmask, _dot_general(be_g, w_g, _TN, prec), 0.0)
                + jnp.where(er == ec, p_c[:, sl_g], 0.0))
        z_bd = jnp.where(bmask, _dot_general(jnp.concatenate([ke_g, be_g], axis=0),
                                             jnp.concatenate([v_g, u0], axis=0), _TN, prec), 0.0)
        s0 = s_ref[gi]
        ys.append(_dot(r_hat, s0, prec) + y0)
        s_ref[gi] = _dot(m_bd, s0, prec) + z_bd
    y = jnp.concatenate(ys, axis=1)

    inv_n = 1.0 / HEAD_DIM
    d = y - head_sum(y) * inv_n
    yn = d * lax.rsqrt(head_sum(d * d) * inv_n + RW_GN_EPS) * lng_ref[...] + lnb_ref[...]
    bonus = head_sum(r * k2 * rk_ref[...]) * v
    o_ref[0] = ((yn + bonus) * g).astype(o_ref.dtype)


def rwkv7(rw, mu, w0, w2, a0, a2, g2, k_k, k_a, r_k, lnx_g, lnx_b, *, prec=3, name="rwkv7"):
    b, s, cols = rw.shape
    gw = GROUP_WIDTH
    zeros = jnp.zeros((DECAY_LORA, gw), F32)
    w2p = jnp.concatenate([w2, zeros], axis=0)
    a2p = jnp.concatenate([zeros, a2], axis=0)
    vec = lambda t: t.reshape(1, -1).astype(F32)
    params = [vec(mu), vec(w0), w2p, vec(a0), a2p, g2, vec(k_k), vec(k_a), vec(r_k), vec(lnx_g), vec(lnx_b)]
    return pl.pallas_call(
        functools.partial(_rwkv_kernel, prec),
        grid=(b, s // RW_CHUNK),
        in_specs=[pl.BlockSpec((1, RW_CHUNK, cols), lambda bi, ci: (bi, ci, 0))]
                 + [pl.BlockSpec(p.shape, lambda bi, ci: (0, 0)) for p in params],
        out_specs=pl.BlockSpec((1, RW_CHUNK, gw), lambda bi, ci: (bi, ci, 0)),
        out_shape=jax.ShapeDtypeStruct((b, s, gw), BF16),
        scratch_shapes=[pltpu.VMEM((8, cols), F32),
                        pltpu.VMEM((gw // RW_GROUP_LANES, RW_GROUP_LANES, RW_GROUP_LANES), F32)],
        compiler_params=_cparams("parallel", "arbitrary"),
        name=name,
    )(rw, *params)


def _tile_lanes(t, n):
    return jnp.concatenate([t] * n, axis=1)


def _expand_matrix():
    e = np.zeros((GROUP_WIDTH, N_HEADS * HEAD_LANES), np.float32)
    for h in range(N_HEADS):
        e[h * HEAD_DIM + np.arange(HEAD_DIM), h * HEAD_LANES + np.arange(HEAD_DIM)] = 1.0
    return jnp.asarray(e, BF16)


def _mla_prep_kernel(x_ref, qn_ref, kvn_ref, wq_ref, wqr_ref, wk_ref, wv_ref,
                     cq_ref, sq_ref, ck_ref, sk_ref, q_ref, k_ref, v_ref):
    x = x_ref[...]
    q_lat = x[:, :MLA_Q_RANK]
    kv_lat = x[:, MLA_Q_RANK:MLA_Q_RANK + MLA_KV_RANK]
    kpe = x[:, MLA_Q_RANK + MLA_KV_RANK:MLA_Q_RANK + MLA_KV_RANK + HEAD_LANES]
    kpe_rot = x[:, MLA_Q_RANK + MLA_KV_RANK + HEAD_LANES:]
    qn = _rms(q_lat, qn_ref[...]).astype(BF16)
    q = (jnp.dot(qn, wq_ref[...], preferred_element_type=F32) * _tile_lanes(cq_ref[...], N_HEADS)
         + jnp.dot(qn, wqr_ref[...], preferred_element_type=F32) * _tile_lanes(sq_ref[...], N_HEADS))
    q_ref[...] = q.astype(BF16)
    kn = _rms(kv_lat, kvn_ref[...]).astype(BF16)
    k_rope = kpe * ck_ref[...] + kpe_rot * sk_ref[...]
    k = jnp.dot(kn, wk_ref[...], preferred_element_type=F32) + _tile_lanes(k_rope, N_HEADS)
    k_ref[...] = k.astype(BF16)
    v_ref[...] = jnp.dot(kn, wv_ref[...], preferred_element_type=F32).astype(BF16)


def mla_prep(x, q_norm, kv_norm, w_uq, w_ukv, seq_len, *, tm=256, name="mla_prep"):
    t = x.shape[0]
    nope, rope, half = HEAD_DIM, MLA_ROPE, MLA_ROPE // 2
    wq = w_uq.reshape(MLA_Q_RANK, N_HEADS, nope + rope)
    wq_pe = wq[:, :, nope:]
    pad = lambda a: jnp.pad(a, ((0, 0), (0, 0), (0, HEAD_LANES - a.shape[2])))
    flat = lambda a: a.reshape(a.shape[0], -1).astype(BF16)
    wq_aug = flat(pad(wq))
    wq_rot = flat(pad(jnp.concatenate([jnp.zeros_like(wq[:, :, :nope]), -wq_pe[:, :, half:], wq_pe[:, :, :half]], axis=2)))
    wkv = w_ukv.reshape(MLA_KV_RANK, N_HEADS, 2 * HEAD_DIM)
    wk_aug = flat(pad(wkv[:, :, :nope]))
    wv = flat(wkv[:, :, nope:])

    scale = float(nope + rope) ** -0.5 * LOG2E
    inv_freq = ROPE_THETA ** (-jnp.arange(0, rope, 2, dtype=F32) / rope)
    ang = jnp.arange(seq_len, dtype=F32)[:, None] * inv_freq[None, :]
    cos2, sin2 = _tile_lanes(jnp.cos(ang), 2), _tile_lanes(jnp.sin(ang), 2)
    z64, z32 = jnp.zeros((seq_len, nope), F32), jnp.zeros((seq_len, HEAD_LANES - nope - rope), F32)
    cq = jnp.concatenate([jnp.full((seq_len, nope), scale, F32), cos2 * scale, z32], axis=1)
    sq = jnp.concatenate([z64, sin2 * scale, z32], axis=1)
    ck = jnp.concatenate([z64, cos2, z32], axis=1)
    sk = jnp.concatenate([z64, sin2, z32], axis=1)

    n_seq_tiles = seq_len // tm
    tab_spec = pl.BlockSpec((tm, HEAD_LANES), lambda i: (i % n_seq_tiles, 0))
    weights = [wq_aug, wq_rot, wk_aug, wv]
    return pl.pallas_call(
        _mla_prep_kernel,
        grid=(t // tm,),
        in_specs=[_row_spec(tm, x.shape[1]), _full_spec((1, MLA_Q_RANK)), _full_spec((1, MLA_KV_RANK))]
                 + [_full_spec(w.shape) for w in weights] + [tab_spec] * 4,
        out_specs=[_row_spec(tm, N_HEADS * HEAD_LANES), _row_spec(tm, N_HEADS * HEAD_LANES),
                   _row_spec(tm, GROUP_WIDTH)],
        out_shape=[jax.ShapeDtypeStruct((t, N_HEADS * HEAD_LANES), BF16),
                   jax.ShapeDtypeStruct((t, N_HEADS * HEAD_LANES), BF16),
                   jax.ShapeDtypeStruct((t, GROUP_WIDTH), BF16)],
        compiler_params=_cparams("parallel"),
        name=name,
    )(x, q_norm.reshape(1, -1), kv_norm.reshape(1, -1), *weights, cq, sq, ck, sk)


FOX_PARTS = 3


def _fox_prep_kernel(scale, q_ref, k_ref, f_ref, bf_ref, e_ref, pq_ref, pk_ref, cq_ref, ck_ref,
                     qo_ref, ko_ref, carry_ref):
    @pl.when(pl.program_id(1) == 0)
    def _():
        carry_ref[...] = jnp.zeros_like(carry_ref)

    tm = q_ref.shape[1]
    log_f = -_softplus(-(f_ref[0] + bf_ref[...]))
    ti = lax.broadcasted_iota(jnp.int32, (tm, tm), 0)
    si = lax.broadcasted_iota(jnp.int32, (tm, tm), 1)
    d = _dot(jnp.where(si <= ti, 1.0, 0.0), log_f, 6) + carry_ref[0:1, :]
    carry_ref[0:1, :] = d[tm - 1:tm, :]
    d2 = d * LOG2E
    hi = d2.astype(BF16)
    rem = d2 - hi.astype(F32)
    mid = rem.astype(BF16)
    lo = (rem - mid.astype(F32)).astype(BF16)
    parts = jnp.concatenate([hi, mid, lo], axis=1)
    q = (jnp.dot((q_ref[0] * scale).astype(BF16), e_ref[...], preferred_element_type=F32)
         + jnp.dot(parts, pq_ref[...], preferred_element_type=F32) + cq_ref[...])
    k = (jnp.dot(k_ref[0].astype(BF16), e_ref[...], preferred_element_type=F32)
         + jnp.dot(parts, pk_ref[...], preferred_element_type=F32) + ck_ref[...])
    qo_ref[0] = q.astype(BF16)
    ko_ref[0] = k.astype(BF16)


def fox_prep(q, k, f_logit, b_f, *, tm=256, name="fox_prep"):
    b, s, _ = q.shape
    width = N_HEADS * HEAD_LANES
    pq = np.zeros((FOX_PARTS * LANES, width), np.float32)
    pk = np.zeros((FOX_PARTS * LANES, width), np.float32)
    cq = np.zeros((1, width), np.float32)
    ck = np.zeros((1, width), np.float32)
    for h in range(N_HEADS):
        base = h * HEAD_LANES + HEAD_DIM
        for p in range(FOX_PARTS):
            pk[p * LANES + h, base + p] = -1.0
            pq[p * LANES + h, base + FOX_PARTS + p] = 1.0
            cq[0, base + p] = 1.0
            ck[0, base + FOX_PARTS + p] = 1.0
    consts = [_expand_matrix(), jnp.asarray(pq, BF16), jnp.asarray(pk, BF16), jnp.asarray(cq), jnp.asarray(ck)]
    bf_pad = jnp.zeros((1, LANES), F32).at[0, :N_HEADS].set(b_f.astype(F32))
    blk = lambda n: pl.BlockSpec((1, tm, n), lambda bi, i: (bi, i, 0))
    return pl.pallas_call(
        functools.partial(_fox_prep_kernel, float(HEAD_DIM) ** -0.5 * LOG2E),
        grid=(b, s // tm),
        in_specs=[blk(GROUP_WIDTH), blk(GROUP_WIDTH), blk(LANES), pl.BlockSpec((1, LANES), lambda bi, i: (0, 0))]
                 + [pl.BlockSpec(c.shape, lambda bi, i: (0, 0)) for c in consts],
        out_specs=[blk(width), blk(width)],
        out_shape=[jax.ShapeDtypeStruct((b, s, width), BF16)] * 2,
        scratch_shapes=[pltpu.VMEM((8, LANES), F32)],
        compiler_params=_cparams("parallel", "arbitrary"),
        name=name,
    )(q, k, f_logit, bf_pad, *consts)


MOBA_NB_PAD = 32


def _rope_full(x, cos, sin):
    half = HEAD_DIM // 2
    lane = lax.broadcasted_iota(jnp.int32, x.shape, 1)
    width = x.shape[1]
    rot = jnp.where(lane % HEAD_DIM < half, -pltpu.roll(x, width - half, axis=1), pltpu.roll(x, half, axis=1))
    return x * cos + rot * sin


def _moba_prep_kernel(q_ref, k_ref, cos_ref, sin_ref, e_ref, qo_ref, ko_ref, km_ref):
    blk = pl.program_id(1)
    cos = _tile_lanes(cos_ref[...], GROUP_WIDTH // LANES)
    sin = _tile_lanes(sin_ref[...], GROUP_WIDTH // LANES)
    qo_ref[0] = _rope_full(q_ref[0], cos, sin)
    k = _rope_full(k_ref[0], cos, sin)
    km_ref[0, 0] = jnp.mean(k, axis=0, keepdims=True)
    lane = lax.broadcasted_iota(jnp.int32, ko_ref.shape[1:], 1) % HEAD_LANES
    own_lane = jnp.where(lane == HEAD_DIM + blk, 1.0, 0.0)
    ko_ref[0] = (jnp.dot(k.astype(BF16), e_ref[...], preferred_element_type=F32) + own_lane).astype(BF16)


def moba_prep(q, k, *, name="moba_prep"):
    b, s, _ = q.shape
    tm = MOBA_BLOCK
    nb = s // tm
    assert nb <= MOBA_NB_PAD
    inv_freq = ROPE_THETA ** (-jnp.arange(0, HEAD_DIM, 2, dtype=F32) / HEAD_DIM)
    ang = jnp.arange(s, dtype=F32)[:, None] * inv_freq[None, :]
    cos, sin = _tile_lanes(jnp.cos(ang), LANES // (HEAD_DIM // 2)), _tile_lanes(jnp.sin(ang), LANES // (HEAD_DIM // 2))
    width = N_HEADS * HEAD_LANES
    blk = lambda n: pl.BlockSpec((1, tm, n), lambda bi, i: (bi, i, 0))
    tab = pl.BlockSpec((tm, LANES), lambda bi, i: (i, 0))
    e = _expand_matrix()
    return pl.pallas_call(
        _moba_prep_kernel,
        grid=(b, nb),
        in_specs=[blk(GROUP_WIDTH), blk(GROUP_WIDTH), tab, tab, pl.BlockSpec(e.shape, lambda bi, i: (0, 0))],
        out_specs=[blk(GROUP_WIDTH), blk(width),
                   pl.BlockSpec((1, 1, 1, GROUP_WIDTH), lambda bi, i: (bi, i, 0, 0))],
        out_shape=[jax.ShapeDtypeStruct((b, s, GROUP_WIDTH), F32), jax.ShapeDtypeStruct((b, s, width), BF16),
                   jax.ShapeDtypeStruct((b, nb, 1, GROUP_WIDTH), F32)],
        compiler_params=_cparams("parallel", "parallel"),
        name=name,
    )(q, k, cos, sin, e)


def _moba_gate_kernel(scale, q_ref, km_ref, e_ref, pm_ref, qo_ref):
    own = pl.program_id(1)
    q = q_ref[0]
    gate_t = _dot_general(km_ref[0], q, _NT, 6)
    nbp = MOBA_NB_PAD
    j = lax.broadcasted_iota(jnp.int32, (nbp, q.shape[0]), 0)
    valid = j < own
    lowest = -3.0e38
    bias_t = []
    for h in range(N_HEADS):
        g = jnp.where(valid, gate_t[h * nbp:(h + 1) * nbp, :], lowest)
        rest = g
        for _ in range(MOBA_TOPK - 1):
            rest = jnp.where(rest >= jnp.max(rest, axis=0, keepdims=True), lowest, rest)
        kth = jnp.max(rest, axis=0, keepdims=True)
        keep = (valid & (g >= kth)) | (j == own)
        bias_t.append(jnp.where(keep, 0.0, NEG_BIG))
    bias = jnp.concatenate(bias_t, axis=0).T
    out = (jnp.dot((q * scale).astype(BF16), e_ref[...], preferred_element_type=F32)
           + jnp.dot(bias.astype(BF16), pm_ref[...], preferred_element_type=F32))
    qo_ref[0] = out.astype(BF16)


def moba_gate(q_rope, k_mean, *, name="moba_gate"):
    b, s, _ = q_rope.shape
    tm = MOBA_BLOCK
    nb = s // tm
    nbp = MOBA_NB_PAD
    width = N_HEADS * HEAD_LANES
    km = k_mean.reshape(b, nb, N_HEADS, HEAD_DIM).transpose(0, 2, 1, 3)
    km = jnp.pad(km, ((0, 0), (0, 0), (0, nbp - nb), (0, 0)))
    eye = jnp.eye(N_HEADS, dtype=F32)
    km_bd = (km[:, :, :, None, :] * eye[None, :, None, :, None]).reshape(b, N_HEADS * nbp, GROUP_WIDTH)
    pm = np.zeros((N_HEADS * nbp, width), np.float32)
    for h in range(N_HEADS):
        pm[h * nbp + np.arange(nbp), h * HEAD_LANES + HEAD_DIM + np.arange(nbp)] = 1.0
    e, pm = _expand_matrix(), jnp.asarray(pm, BF16)
    return pl.pallas_call(
        functools.partial(_moba_gate_kernel, float(HEAD_DIM) ** -0.5 * LOG2E),
        grid=(b, nb),
        in_specs=[pl.BlockSpec((1, tm, GROUP_WIDTH), lambda bi, i: (bi, i, 0)),
                  pl.BlockSpec((1, N_HEADS * nbp, GROUP_WIDTH), lambda bi, i: (bi, 0, 0)),
                  pl.BlockSpec(e.shape, lambda bi, i: (0, 0)), pl.BlockSpec(pm.shape, lambda bi, i: (0, 0))],
        out_specs=pl.BlockSpec((1, tm, width), lambda bi, i: (bi, i, 0)),
        out_shape=jax.ShapeDtypeStruct((b, s, width), BF16),
        compiler_params=_cparams("parallel", "parallel"),
        name=name,
    )(q_rope, km_bd, e, pm)


def kernel(x, norm_mix_0, w_in_0, shift_mu_0, rw_w0_0, rw_w2_0, rw_a0_0, rw_a2_0, rw_g2_0, rw_kk_0, rw_ka_0,
           rw_rk_0, rw_lnx_g_0, rw_lnx_b_0, mla_qnorm_0, mla_wuq_0, mla_kvnorm_0, mla_wukv_0, w_out_0,
           norm_ffn_0, ffn_wg_0, ffn_wu_0, ffn_wd_0, norm_mix_1, w_in_1, fox_bf_1, w_out_1, norm_ffn_1,
           router_1, moe_wg_1, moe_wu_1, moe_wd_1, final_norm):
    b, s, d = x.shape
    t = b * s
    gw = GROUP_WIDTH
    bf = lambda w: w.astype(BF16)
    x2 = x.reshape(t, d)

    mla0 = RW_COLS
    w_q, w_kv = w_in_0[:, mla0:mla0 + MLA_Q_RANK], w_in_0[:, mla0 + MLA_Q_RANK:mla0 + MLA_Q_RANK + MLA_KV_RANK]
    w_kr = w_in_0[:, mla0 + MLA_Q_RANK + MLA_KV_RANK:]
    half = MLA_ROPE // 2
    z = lambda n: jnp.zeros((d, n), F32)
    w_kpe = jnp.concatenate([z(HEAD_DIM), w_kr, z(HEAD_LANES - HEAD_DIM - MLA_ROPE)], axis=1)
    w_kpe_rot = jnp.concatenate([z(HEAD_DIM), -w_kr[:, half:], w_kr[:, :half],
                                 z(HEAD_LANES - HEAD_DIM - MLA_ROPE)], axis=1)
    w_mla = jnp.concatenate([w_q, w_kv, w_kpe, w_kpe_rot], axis=1)
    rw, mla = norm_proj(x2, norm_mix_0, [bf(w_in_0[:, :RW_COLS]), bf(w_mla)], [F32, F32], name="in_proj_0")
    y_a = rwkv7(rw.reshape(b, s, RW_COLS), shift_mu_0, rw_w0_0, rw_w2_0, rw_a0_0, rw_a2_0, rw_g2_0,
                rw_kk_0, rw_ka_0, rw_rk_0, rw_lnx_g_0, rw_lnx_b_0)
    q, k, v = mla_prep(mla, mla_qnorm_0, mla_kvnorm_0, mla_wuq_0, mla_wukv_0, s)
    y_b = flash_attention(q.reshape(b, s, -1), k.reshape(b, s, -1), v.reshape(b, s, -1), name="flash_mla")
    h = out_proj(y_a.reshape(t, gw), y_b.reshape(t, gw), bf(w_out_0[:gw]), bf(w_out_0[gw:]), x2, name="out_proj_0")
    h = ffn(h, norm_ffn_0, bf(ffn_wg_0), bf(ffn_wu_0), bf(ffn_wd_0))

    c0 = 3 * gw
    w_f = jnp.pad(w_in_1[:, c0:c0 + N_HEADS], ((0, 0), (0, LANES - N_HEADS)))
    c1 = c0 + N_HEADS
    cols = [w_in_1[:, 0:gw], w_in_1[:, gw:2 * gw], w_in_1[:, 2 * gw:3 * gw], w_f,
            w_in_1[:, c1:c1 + gw], w_in_1[:, c1 + gw:c1 + 2 * gw], w_in_1[:, c1 + 2 * gw:]]
    fq, fk, fv, ff, mq, mk, mv = norm_proj(h, norm_mix_1, [bf(w) for w in cols],
                                           [F32, F32, BF16, F32, F32, F32, BF16], name="in_proj_1")
    r3 = lambda a: a.reshape(b, s, -1)
    fqa, fka = fox_prep(r3(fq), r3(fk), r3(ff), fox_bf_1)
    y_c = flash_attention(fqa, fka, r3(fv), name="flash_fox")
    mq_rope, mka, k_mean = moba_prep(r3(mq), r3(mk))
    mqa = moba_gate(mq_rope, k_mean)
    y_d = flash_attention(mqa, mka, r3(mv), name="flash_moba")
    h = out_proj(y_c.reshape(t, gw), y_d.reshape(t, gw), bf(w_out_1[:gw]), bf(w_out_1[gw:]), h, name="out_proj_1")

    router_pad = jnp.pad(router_1, ((0, 0), (0, LANES - N_EXPERTS)))
    out = moe(h, norm_ffn_1, router_pad, bf(moe_wg_1), bf(moe_wu_1), bf(moe_wd_1), final_norm)
    return out.reshape(b, s, d)
```

```python
import functools

import numpy as np
import jax
import jax.numpy as jnp
from jax import lax
from jax.experimental import pallas as pl
from jax.experimental.pallas import tpu as pltpu

F32 = jnp.float32
BF16 = jnp.bfloat16

HEAD_DIM = 64
N_HEADS = 8
GROUP_WIDTH = N_HEADS * HEAD_DIM
HEAD_LANES = 128
MLA_ROPE = 32
MLA_Q_RANK = 256
MLA_KV_RANK = 128
DECAY_LORA = 64
AAA_LORA = 64
GATE_LORA = 128
RW_COLS = 3 * GROUP_WIDTH + DECAY_LORA + AAA_LORA + GATE_LORA
RW_GN_EPS = 64e-5
MOBA_BLOCK = 256
MOBA_TOPK = 3
ROPE_THETA = 10000.0
NORM_EPS = 1e-6
N_EXPERTS = 8
NEG_BIG = -1e30

LANES = 128
VMEM_LIMIT_BYTES = 56 * 1024 * 1024

RW_CHUNK = 64
RW_GROUP_HEADS = 4
RW_GROUP_LANES = RW_GROUP_HEADS * HEAD_DIM


def _cparams(*semantics):
    return pltpu.CompilerParams(dimension_semantics=semantics, vmem_limit_bytes=VMEM_LIMIT_BYTES)


def _dot(a, b, prec=1):
    if prec == 6:
        return jnp.dot(a.astype(F32), b.astype(F32), preferred_element_type=F32,
                       precision=lax.Precision.HIGHEST)
    if prec == 1:
        return jnp.dot(a.astype(BF16), b.astype(BF16), preferred_element_type=F32)
    ah, al = _split(a)
    if b.dtype == BF16:
        return jnp.dot(ah, b, preferred_element_type=F32) + jnp.dot(al, b, preferred_element_type=F32)
    bh, bl = _split(b)
    return (jnp.dot(ah, bh, preferred_element_type=F32) + jnp.dot(ah, bl, preferred_element_type=F32)
            + jnp.dot(al, bh, preferred_element_type=F32))


def _split(x):
    hi = x.astype(BF16)
    return hi, (x - hi.astype(F32)).astype(BF16)


_NT = (((1,), (1,)), ((), ()))
_TN = (((0,), (0,)), ((), ()))


def _dot_general(a, b, dims, prec=1):
    if prec == 6:
        return lax.dot_general(a.astype(F32), b.astype(F32), dims, preferred_element_type=F32,
                               precision=lax.Precision.HIGHEST)
    if prec == 1:
        return lax.dot_general(a.astype(BF16), b.astype(BF16), dims, preferred_element_type=F32)
    ah, al = _split(a)
    bh, bl = _split(b)
    return (lax.dot_general(ah, bh, dims, preferred_element_type=F32)
            + lax.dot_general(ah, bl, dims, preferred_element_type=F32)
            + lax.dot_general(al, bh, dims, preferred_element_type=F32))


def _rms(x, g):
    return x * lax.rsqrt(jnp.mean(x * x, axis=-1, keepdims=True) + NORM_EPS) * g


def _softplus(x):
    return jnp.maximum(x, 0.0) + jnp.log(1.0 + jnp.exp(-jnp.abs(x)))


def _sigmoid(x):
    return 1.0 / (1.0 + jnp.exp(-x))


def _row_spec(tm, n):
    return pl.BlockSpec((tm, n), lambda i: (i, 0))


def _full_spec(shape):
    nd = len(shape)
    return pl.BlockSpec(shape, lambda *_: (0,) * nd)


def _norm_proj_kernel(n_out, x_ref, g_ref, *refs):
    w_refs, o_refs = refs[:n_out], refs[n_out:]
    xn = _rms(x_ref[...], g_ref[...]).astype(BF16)
    for w_ref, o_ref in zip(w_refs, o_refs):
        o_ref[...] = jnp.dot(xn, w_ref[...], preferred_element_type=F32).astype(o_ref.dtype)


def norm_proj(x, g, weights, dtypes, *, tm=256, name="norm_proj"):
    t, d = x.shape
    n_out = len(weights)
    return pl.pallas_call(
        functools.partial(_norm_proj_kernel, n_out),
        grid=(t // tm,),
        in_specs=[_row_spec(tm, d), _full_spec((1, d))] + [_full_spec(w.shape) for w in weights],
        out_specs=[_row_spec(tm, w.shape[1]) for w in weights],
        out_shape=[jax.ShapeDtypeStruct((t, w.shape[1]), dt) for w, dt in zip(weights, dtypes)],
        compiler_params=_cparams("parallel"),
        name=name,
    )(x, g.reshape(1, d), *weights)


def _out_proj_kernel(y1_ref, y2_ref, w1_ref, w2_ref, r_ref, o_ref):
    o_ref[...] = (r_ref[...] + jnp.dot(y1_ref[...], w1_ref[...], preferred_element_type=F32)
                  + jnp.dot(y2_ref[...], w2_ref[...], preferred_element_type=F32))


def out_proj(y1, y2, w1, w2, resid, *, tm=512, name="out_proj"):
    t, d = resid.shape
    return pl.pallas_call(
        _out_proj_kernel,
        grid=(t // tm,),
        in_specs=[_row_spec(tm, y1.shape[1]), _row_spec(tm, y2.shape[1]),
                  _full_spec(w1.shape), _full_spec(w2.shape), _row_spec(tm, d)],
        out_specs=_row_spec(tm, d),
        out_shape=jax.ShapeDtypeStruct((t, d), F32),
        compiler_params=_cparams("parallel"),
        name=name,
    )(y1, y2, w1, w2, resid)


def _ffn_kernel(h_ref, g_ref, wg_ref, wu_ref, wd_ref, o_ref, xn_ref, acc_ref):
    f = pl.program_id(1)

    @pl.when(f == 0)
    def _():
        xn_ref[...] = _rms(h_ref[...], g_ref[...]).astype(BF16)
        acc_ref[...] = jnp.zeros_like(acc_ref)

    xn = xn_ref[...]
    gate = jnp.dot(xn, wg_ref[...], preferred_element_type=F32)
    up = jnp.dot(xn, wu_ref[...], preferred_element_type=F32)
    act = (gate * _sigmoid(gate) * up).astype(BF16)
    acc_ref[...] += jnp.dot(act, wd_ref[...], preferred_element_type=F32)

    @pl.when(f == pl.num_programs(1) - 1)
    def _():
        o_ref[...] = h_ref[...] + acc_ref[...]


def ffn(h, g, wg, wu, wd, *, tm=1024, tf=256, name="ffn"):
    t, d = h.shape
    dff = wg.shape[1]
    return pl.pallas_call(
        _ffn_kernel,
        grid=(t // tm, dff // tf),
        in_specs=[pl.BlockSpec((tm, d), lambda i, f: (i, 0)),
                  pl.BlockSpec((1, d), lambda i, f: (0, 0)),
                  pl.BlockSpec((d, tf), lambda i, f: (0, f)),
                  pl.BlockSpec((d, tf), lambda i, f: (0, f)),
                  pl.BlockSpec((tf, d), lambda i, f: (f, 0))],
        out_specs=pl.BlockSpec((tm, d), lambda i, f: (i, 0)),
        out_shape=jax.ShapeDtypeStruct((t, d), F32),
        scratch_shapes=[pltpu.VMEM((tm, d), BF16), pltpu.VMEM((tm, d), F32)],
        compiler_params=_cparams("parallel", "arbitrary"),
        name=name,
    )(h, g.reshape(1, d), wg, wu, wd)


def _moe_kernel(h_ref, g_ref, router_ref, wg_ref, wu_ref, wd_ref, fn_ref, o_ref,
                xn_ref, acc_ref, comb_ref, ce_ref):
    e = pl.program_id(1)
    f = pl.program_id(2)
    lane = lax.broadcasted_iota(jnp.int32, comb_ref.shape, 1)

    @pl.when((e == 0) & (f == 0))
    def _():
        t = _rms(h_ref[...], g_ref[...])
        xn_ref[...] = t.astype(BF16)
        acc_ref[...] = jnp.zeros_like(acc_ref)
        logits = _dot(t, router_ref[...], prec=6)
        logits = jnp.where(lane < N_EXPERTS, logits, NEG_BIG)
        m1 = jnp.max(logits, axis=-1, keepdims=True)
        i1 = jnp.min(jnp.where(logits == m1, lane, LANES), axis=-1, keepdims=True)
        rest = jnp.where(lane == i1, NEG_BIG, logits)
        m2 = jnp.max(rest, axis=-1, keepdims=True)
        i2 = jnp.min(jnp.where(rest == m2, lane, LANES), axis=-1, keepdims=True)
        ex = jnp.exp(m2 - m1)
        g1 = 1.0 / (1.0 + ex)
        comb_ref[...] = jnp.where(lane == i1, g1, jnp.where(lane == i2, ex * g1, 0.0))

    @pl.when(f == 0)
    def _():
        ce_ref[...] = jnp.sum(jnp.where(lane == e, comb_ref[...], 0.0), axis=-1, keepdims=True)

    xn = xn_ref[...]
    gate = jnp.dot(xn, wg_ref[0], preferred_element_type=F32)
    up = jnp.dot(xn, wu_ref[0], preferred_element_type=F32)
    act = (gate * _sigmoid(gate) * up * ce_ref[...]).astype(BF16)
    acc_ref[...] += jnp.dot(act, wd_ref[0], preferred_element_type=F32)

    @pl.when((e == pl.num_programs(1) - 1) & (f == pl.num_programs(2) - 1))
    def _():
        o_ref[...] = _rms(h_ref[...] + acc_ref[...], fn_ref[...])


def moe(h, g, router_pad, wg, wu, wd, final_g, *, tm=1024, tf=512, name="moe"):
    t, d = h.shape
    n_e, _, dff = wg.shape
    return pl.pallas_call(
        _moe_kernel,
        grid=(t // tm, n_e, dff // tf),
        in_specs=[pl.BlockSpec((tm, d), lambda i, e, f: (i, 0)),
                  pl.BlockSpec((1, d), lambda i, e, f: (0, 0)),
                  pl.BlockSpec((d, LANES), lambda i, e, f: (0, 0)),
                  pl.BlockSpec((1, d, tf), lambda i, e, f: (e, 0, f)),
                  pl.BlockSpec((1, d, tf), lambda i, e, f: (e, 0, f)),
                  pl.BlockSpec((1, tf, d), lambda i, e, f: (e, f, 0)),
                  pl.BlockSpec((1, d), lambda i, e, f: (0, 0))],
        out_specs=pl.BlockSpec((tm, d), lambda i, e, f: (i, 0)),
        out_shape=jax.ShapeDtypeStruct((t, d), F32),
        scratch_shapes=[pltpu.VMEM((tm, d), BF16), pltpu.VMEM((tm, d), F32),
                        pltpu.VMEM((tm, LANES), F32), pltpu.VMEM((tm, 1), F32)],
        compiler_params=_cparams("parallel", "arbitrary", "arbitrary"),
        name=name,
    )(h, g.reshape(1, d), router_pad, wg, wu, wd, final_g.reshape(1, d))


LOG2E = 1.4426950408889634


def _flash_kernel(blk, q_ref, k_ref, vt_ref, o_ref, acc_ref):
    i = pl.program_id(2)
    key = lax.broadcasted_iota(jnp.int32, (blk, blk), 0)
    qry = lax.broadcasted_iota(jnp.int32, (blk, blk), 1)
    causal = key <= qry

    def scores(h, kb):
        start = pl.multiple_of(kb * blk, blk)
        k = k_ref[0, pl.ds(start, blk), h * HEAD_LANES:(h + 1) * HEAD_LANES]
        q = q_ref[0, :, h * HEAD_LANES:(h + 1) * HEAD_LANES]
        return lax.dot_general(k, q, _NT, preferred_element_type=F32)

    both = [jnp.where(causal, scores(h, i), NEG_BIG) for h in range(2)]
    ms = []
    for h in range(2):
        m = jnp.max(both[h], axis=0, keepdims=True)
        p = jnp.exp2((both[h] - m).astype(BF16))
        ms.append(m)
        acc_ref[h] = jnp.dot(vt_ref[0, h, i], p, preferred_element_type=F32)

    def body(kb, carry):
        both = [scores(h, kb) for h in range(2)]
        new = []
        for h in range(2):
            m_new = jnp.maximum(carry[h], jnp.max(both[h], axis=0, keepdims=True))
            alpha = jnp.exp2(carry[h] - m_new)
            p = jnp.exp2((both[h] - m_new).astype(BF16))
            new.append(m_new)
            acc_ref[h] = alpha * acc_ref[h] + jnp.dot(vt_ref[0, h, kb], p, preferred_element_type=F32)
        return tuple(new)

    lax.fori_loop(0, i, body, tuple(ms))
    outs = []
    for h in range(2):
        acc = acc_ref[h]
        outs.append(acc[:HEAD_DIM] / acc[HEAD_DIM:HEAD_DIM + 1])
    o_ref[0] = jnp.concatenate(outs, axis=0).T.astype(o_ref.dtype)


FLASH_V_ROWS = 80


def flash_attention(q, k, v, *, blk=512, name="flash"):
    b, s, hw = q.shape
    n_heads = hw // HEAD_LANES
    nb = s // blk
    vt = v.reshape(b, nb, blk, n_heads, HEAD_DIM).transpose(0, 3, 1, 4, 2)
    pad_rows = FLASH_V_ROWS - HEAD_DIM - 1
    vt = jnp.concatenate([vt, jnp.ones((b, n_heads, nb, 1, blk), BF16),
                          jnp.zeros((b, n_heads, nb, pad_rows, blk), BF16)], axis=3)
    return pl.pallas_call(
        functools.partial(_flash_kernel, blk),
        grid=(b, n_heads // 2, nb),
        in_specs=[pl.BlockSpec((1, blk, 2 * HEAD_LANES), lambda bi, p, i: (bi, i, p)),
                  pl.BlockSpec((1, s, 2 * HEAD_LANES), lambda bi, p, i: (bi, 0, p)),
                  pl.BlockSpec((1, 2, nb, FLASH_V_ROWS, blk), lambda bi, p, i: (bi, p, 0, 0, 0))],
        out_specs=pl.BlockSpec((1, blk, 2 * HEAD_DIM), lambda bi, p, i: (bi, i, p)),
        out_shape=jax.ShapeDtypeStruct((b, s, n_heads * HEAD_DIM), BF16),
        scratch_shapes=[pltpu.VMEM((2, FLASH_V_ROWS, blk), F32)],
        compiler_params=_cparams("parallel", "parallel", "arbitrary"),
        name=name,
    )(q, k, vt)


def _rwkv_kernel(prec, x_ref, mu_ref, w0_ref, w2_ref, a0_ref, a2_ref, g2_ref, kk_ref, ka_ref, rk_ref,
                 lng_ref, lnb_ref, o_ref, carry_ref, s_ref):
    c = RW_CHUNK
    gl = RW_GROUP_LANES
    n_groups = GROUP_WIDTH // gl

    @pl.when(pl.program_id(1) == 0)
    def _():
        carry_ref[...] = jnp.zeros_like(carry_ref)
        s_ref[...] = jnp.zeros_like(s_ref)

    x = x_ref[0]
    row = lax.broadcasted_iota(jnp.int32, x.shape, 0)
    prev = jnp.where(row == 0, carry_ref[0:1, :], pltpu.roll(x, 1, axis=0))
    carry_ref[0:1, :] = x[c - 1:c, :]
    xs = x + mu_ref[...] * (prev - x)

    gw = GROUP_WIDTH
    r, k, v = xs[:, 0:gw], xs[:, gw:2 * gw], xs[:, 2 * gw:3 * gw]
    wa = xs[:, 3 * gw:3 * gw + DECAY_LORA + AAA_LORA]
    gd = xs[:, 3 * gw + DECAY_LORA + AAA_LORA:]
    log_w = -_softplus(-(w0_ref[...] + _dot(jnp.tanh(wa), w2_ref[...], 3))) - 0.5
    lw = -jnp.exp(log_w)
    a = _sigmoid(a0_ref[...] + _dot(wa, a2_ref[...], 3))
    g = _dot(_sigmoid(gd), g2_ref[...], 3)

    brow = lax.broadcasted_iota(jnp.int32, (gl, gl), 0) // HEAD_DIM
    bcol = lax.broadcasted_iota(jnp.int32, (gl, gl), 1) // HEAD_DIM
    bmask = brow == bcol
    bones = jnp.where(bmask, 1.0, 0.0).astype(BF16)

    def head_sum(t):
        return jnp.concatenate([_dot(t[:, i * gl:(i + 1) * gl], bones, 2) for i in range(n_groups)], axis=1)

    kk = k * kk_ref[...]
    kk = kk / jnp.maximum(jnp.sqrt(head_sum(kk * kk)), 1e-12)
    k2 = k * (1.0 + (a - 1.0) * ka_ref[...])
    b = kk * a

    ti = lax.broadcasted_iota(jnp.int32, (c, c), 0)
    si = lax.broadcasted_iota(jnp.int32, (c, c), 1)
    log_p = _dot(jnp.where(si <= ti, 1.0, 0.0), lw, 6)
    log_pc = log_p[c - 1:c, :]
    inv_p = jnp.exp(-log_p)
    to_end = jnp.exp(log_pc - log_p)
    a_t = -kk * jnp.exp(log_p - lw)
    r_t = r * jnp.exp(log_p)
    b_t, k_t = b * inv_p, k2 * inv_p
    b_e, k_e = b * to_end, k2 * to_end
    p_c = jnp.exp(log_pc)

    tl = lax.broadcasted_iota(jnp.int32, (c, gl), 0)
    sl = lax.broadcasted_iota(jnp.int32, (c, gl), 1) % HEAD_DIM
    strict, incl = sl < tl, sl <= tl
    eye_l = jnp.where(sl == tl, 1.0, 0.0)
    er = lax.broadcasted_iota(jnp.int32, (gl, gl), 0)
    ec = lax.broadcasted_iota(jnp.int32, (gl, gl), 1)

    def bd(t):
        return jnp.where(bmask, jnp.concatenate([t] * RW_GROUP_HEADS, axis=0), 0.0)

    ys = []
    for gi in range(n_groups):
        sl_g = slice(gi * gl, (gi + 1) * gl)
        at_g, rt_g, bt_g, kt_g = a_t[:, sl_g], r_t[:, sl_g], b_t[:, sl_g], k_t[:, sl_g]
        be_g, ke_g, v_g = b_e[:, sl_g], k_e[:, sl_g], v[:, sl_g]
        gram = _dot_general(jnp.concatenate([at_g, rt_g], axis=0),
                            jnp.concatenate([bd(bt_g), bd(kt_g)], axis=0), _NT, prec)
        a_ab = jnp.where(strict, gram[:c, :gl], 0.0)
        a_ak = jnp.where(strict, gram[:c, gl:], 0.0)
        a_rb = jnp.where(incl, gram[c:, :gl], 0.0)
        a_rk = jnp.where(incl, gram[c:, gl:], 0.0)
        pw = _dot(a_ab, bd(a_ab), prec)
        tinv = eye_l + a_ab
        n_sq = int(np.log2(c)) - 1
        for it in range(n_sq):
            if it + 1 < n_sq:
                both = _dot(jnp.concatenate([tinv, pw], axis=0), bd(pw), prec)
                tinv, pw = tinv + both[:c], both[c:]
            else:
                tinv = tinv + _dot(tinv, bd(pw), prec)
        akv = _dot(jnp.concatenate([a_ak, a_rk], axis=0), bd(v_g), prec)
        wu = _dot(tinv, jnp.concatenate([bd(at_g), bd(akv[:c])], axis=1), prec)
        w_g, u0 = wu[:, :gl], wu[:, gl:]
        rwu = _dot(a_rb, jnp.concatenate([bd(w_g), bd(u0)], axis=1), prec)
        r_hat = rt_g + rwu[:, :gl]
        y0 = akv[c:] + rwu[:, gl:]
        m_bd = (jnp.where(bmask, _dot_general(be_g, w_g, _TN, prec), 0.0)
                + jnp.where(er == ec, p_c[:, sl_g], 0.0))
        z_bd = jnp.where(bmask, _dot_general(jnp.concatenate([ke_g, be_g], axis=0),
                                             jnp.concatenate([v_g, u0], axis=0), _TN, prec), 0.0)
        s0 = s_ref[gi]
        ys.append(_dot(r_hat, s0, 3) + y0)
        s_ref[gi] = _dot(m_bd, s0, 3) + z_bd
    y = jnp.concatenate(ys, axis=1)

    inv_n = 1.0 / HEAD_DIM
    d = y - head_sum(y) * inv_n
    yn = d * lax.rsqrt(head_sum(d * d) * inv_n + RW_GN_EPS) * lng_ref[...] + lnb_ref[...]
    bonus = head_sum(r * k2 * rk_ref[...]) * v
    o_ref[0] = ((yn + bonus) * g).astype(o_ref.dtype)


def rwkv7(rw, mu, w0, w2, a0, a2, g2, k_k, k_a, r_k, lnx_g, lnx_b, *, prec=1, name="rwkv7"):
    b, s, cols = rw.shape
    gw = GROUP_WIDTH
    zeros = jnp.zeros((DECAY_LORA, gw), F32)
    w2p = jnp.concatenate([w2, zeros], axis=0)
    a2p = jnp.concatenate([zeros, a2], axis=0)
    vec = lambda t: t.reshape(1, -1).astype(F32)
    params = [vec(mu), vec(w0), w2p, vec(a0), a2p, g2, vec(k_k), vec(k_a), vec(r_k), vec(lnx_g), vec(lnx_b)]
    return pl.pallas_call(
        functools.partial(_rwkv_kernel, prec),
        grid=(b, s // RW_CHUNK),
        in_specs=[pl.BlockSpec((1, RW_CHUNK, cols), lambda bi, ci: (bi, ci, 0))]
                 + [pl.BlockSpec(p.shape, lambda bi, ci: (0, 0)) for p in params],
        out_specs=pl.BlockSpec((1, RW_CHUNK, gw), lambda bi, ci: (bi, ci, 0)),
        out_shape=jax.ShapeDtypeStruct((b, s, gw), BF16),
        scratch_shapes=[pltpu.VMEM((8, cols), F32),
                        pltpu.VMEM((gw // RW_GROUP_LANES, RW_GROUP_LANES, RW_GROUP_LANES), F32)],
        compiler_params=_cparams("parallel", "arbitrary"),
        name=name,
    )(rw, *params)


def _tile_lanes(t, n):
    return jnp.concatenate([t] * n, axis=1)


def _expand_matrix():
    e = np.zeros((GROUP_WIDTH, N_HEADS * HEAD_LANES), np.float32)
    for h in range(N_HEADS):
        e[h * HEAD_DIM + np.arange(HEAD_DIM), h * HEAD_LANES + np.arange(HEAD_DIM)] = 1.0
    return jnp.asarray(e, BF16)


def _mla_prep_kernel(x_ref, qn_ref, kvn_ref, wq_ref, wqr_ref, wk_ref, wv_ref,
                     cq_ref, sq_ref, ck_ref, sk_ref, q_ref, k_ref, v_ref):
    x = x_ref[...]
    q_lat = x[:, :MLA_Q_RANK]
    kv_lat = x[:, MLA_Q_RANK:MLA_Q_RANK + MLA_KV_RANK]
    kpe = x[:, MLA_Q_RANK + MLA_KV_RANK:MLA_Q_RANK + MLA_KV_RANK + HEAD_LANES]
    kpe_rot = x[:, MLA_Q_RANK + MLA_KV_RANK + HEAD_LANES:]
    qn = _rms(q_lat, qn_ref[...]).astype(BF16)
    q = (jnp.dot(qn, wq_ref[...], preferred_element_type=F32) * _tile_lanes(cq_ref[...], N_HEADS)
         + jnp.dot(qn, wqr_ref[...], preferred_element_type=F32) * _tile_lanes(sq_ref[...], N_HEADS))
    q_ref[...] = q.astype(BF16)
    kn = _rms(kv_lat, kvn_ref[...]).astype(BF16)
    k_rope = kpe * ck_ref[...] + kpe_rot * sk_ref[...]
    k = jnp.dot(kn, wk_ref[...], preferred_element_type=F32) + _tile_lanes(k_rope, N_HEADS)
    k_ref[...] = k.astype(BF16)
    v_ref[...] = jnp.dot(kn, wv_ref[...], preferred_element_type=F32).astype(BF16)


def mla_prep(x, q_norm, kv_norm, w_uq, w_ukv, seq_len, *, tm=256, name="mla_prep"):
    t = x.shape[0]
    nope, rope, half = HEAD_DIM, MLA_ROPE, MLA_ROPE // 2
    wq = w_uq.reshape(MLA_Q_RANK, N_HEADS, nope + rope)
    wq_pe = wq[:, :, nope:]
    pad = lambda a: jnp.pad(a, ((0, 0), (0, 0), (0, HEAD_LANES - a.shape[2])))
    flat = lambda a: a.reshape(a.shape[0], -1).astype(BF16)
    wq_aug = flat(pad(wq))
    wq_rot = flat(pad(jnp.concatenate([jnp.zeros_like(wq[:, :, :nope]), -wq_pe[:, :, half:], wq_pe[:, :, :half]], axis=2)))
    wkv = w_ukv.reshape(MLA_KV_RANK, N_HEADS, 2 * HEAD_DIM)
    wk_aug = flat(pad(wkv[:, :, :nope]))
    wv = flat(wkv[:, :, nope:])

    scale = float(nope + rope) ** -0.5 * LOG2E
    inv_freq = ROPE_THETA ** (-jnp.arange(0, rope, 2, dtype=F32) / rope)
    ang = jnp.arange(seq_len, dtype=F32)[:, None] * inv_freq[None, :]
    cos2, sin2 = _tile_lanes(jnp.cos(ang), 2), _tile_lanes(jnp.sin(ang), 2)
    z64, z32 = jnp.zeros((seq_len, nope), F32), jnp.zeros((seq_len, HEAD_LANES - nope - rope), F32)
    cq = jnp.concatenate([jnp.full((seq_len, nope), scale, F32), cos2 * scale, z32], axis=1)
    sq = jnp.concatenate([z64, sin2 * scale, z32], axis=1)
    ck = jnp.concatenate([z64, cos2, z32], axis=1)
    sk = jnp.concatenate([z64, sin2, z32], axis=1)

    n_seq_tiles = seq_len // tm
    tab_spec = pl.BlockSpec((tm, HEAD_LANES), lambda i: (i % n_seq_tiles, 0))
    weights = [wq_aug, wq_rot, wk_aug, wv]
    return pl.pallas_call(
        _mla_prep_kernel,
        grid=(t // tm,),
        in_specs=[_row_spec(tm, x.shape[1]), _full_spec((1, MLA_Q_RANK)), _full_spec((1, MLA_KV_RANK))]
                 + [_full_spec(w.shape) for w in weights] + [tab_spec] * 4,
        out_specs=[_row_spec(tm, N_HEADS * HEAD_LANES), _row_spec(tm, N_HEADS * HEAD_LANES),
                   _row_spec(tm, GROUP_WIDTH)],
        out_shape=[jax.ShapeDtypeStruct((t, N_HEADS * HEAD_LANES), BF16),
                   jax.ShapeDtypeStruct((t, N_HEADS * HEAD_LANES), BF16),
                   jax.ShapeDtypeStruct((t, GROUP_WIDTH), BF16)],
        compiler_params=_cparams("parallel"),
        name=name,
    )(x, q_norm.reshape(1, -1), kv_norm.reshape(1, -1), *weights, cq, sq, ck, sk)


FOX_PARTS = 3


def _fox_prep_kernel(scale, q_ref, k_ref, f_ref, bf_ref, e_ref, pq_ref, pk_ref, cq_ref, ck_ref,
                     qo_ref, ko_ref, carry_ref):
    @pl.when(pl.program_id(1) == 0)
    def _():
        carry_ref[...] = jnp.zeros_like(carry_ref)

    tm = q_ref.shape[1]
    log_f = -_softplus(-(f_ref[0] + bf_ref[...]))
    ti = lax.broadcasted_iota(jnp.int32, (tm, tm), 0)
    si = lax.broadcasted_iota(jnp.int32, (tm, tm), 1)
    d = _dot(jnp.where(si <= ti, 1.0, 0.0), log_f, 6) + carry_ref[0:1, :]
    carry_ref[0:1, :] = d[tm - 1:tm, :]
    d2 = d * LOG2E
    hi = d2.astype(BF16)
    rem = d2 - hi.astype(F32)
    mid = rem.astype(BF16)
    lo = (rem - mid.astype(F32)).astype(BF16)
    parts = jnp.concatenate([hi, mid, lo], axis=1)
    q = (jnp.dot((q_ref[0] * scale).astype(BF16), e_ref[...], preferred_element_type=F32)
         + jnp.dot(parts, pq_ref[...], preferred_element_type=F32) + cq_ref[...])
    k = (jnp.dot(k_ref[0].astype(BF16), e_ref[...], preferred_element_type=F32)
         + jnp.dot(parts, pk_ref[...], preferred_element_type=F32) + ck_ref[...])
    qo_ref[0] = q.astype(BF16)
    ko_ref[0] = k.astype(BF16)


def fox_prep(q, k, f_logit, b_f, *, tm=256, name="fox_prep"):
    b, s, _ = q.shape
    width = N_HEADS * HEAD_LANES
    pq = np.zeros((FOX_PARTS * LANES, width), np.float32)
    pk = np.zeros((FOX_PARTS * LANES, width), np.float32)
    cq = np.zeros((1, width), np.float32)
    ck = np.zeros((1, width), np.float32)
    for h in range(N_HEADS):
        base = h * HEAD_LANES + HEAD_DIM
        for p in range(FOX_PARTS):
            pk[p * LANES + h, base + p] = -1.0
            pq[p * LANES + h, base + FOX_PARTS + p] = 1.0
            cq[0, base + p] = 1.0
            ck[0, base + FOX_PARTS + p] = 1.0
    consts = [_expand_matrix(), jnp.asarray(pq, BF16), jnp.asarray(pk, BF16), jnp.asarray(cq), jnp.asarray(ck)]
    bf_pad = jnp.zeros((1, LANES), F32).at[0, :N_HEADS].set(b_f.astype(F32))
    blk = lambda n: pl.BlockSpec((1, tm, n), lambda bi, i: (bi, i, 0))
    return pl.pallas_call(
        functools.partial(_fox_prep_kernel, float(HEAD_DIM) ** -0.5 * LOG2E),
        grid=(b, s // tm),
        in_specs=[blk(GROUP_WIDTH), blk(GROUP_WIDTH), blk(LANES), pl.BlockSpec((1, LANES), lambda bi, i: (0, 0))]
                 + [pl.BlockSpec(c.shape, lambda bi, i: (0, 0)) for c in consts],
        out_specs=[blk(width), blk(width)],
        out_shape=[jax.ShapeDtypeStruct((b, s, width), BF16)] * 2,
        scratch_shapes=[pltpu.VMEM((8, LANES), F32)],
        compiler_params=_cparams("parallel", "arbitrary"),
        name=name,
    )(q, k, f_logit, bf_pad, *consts)


MOBA_NB_PAD = 32


def _rope_full(x, cos, sin):
    half = HEAD_DIM // 2
    lane = lax.broadcasted_iota(jnp.int32, x.shape, 1)
    width = x.shape[1]
    rot = jnp.where(lane % HEAD_DIM < half, -pltpu.roll(x, width - half, axis=1), pltpu.roll(x, half, axis=1))
    return x * cos + rot * sin


def _moba_prep_kernel(q_ref, k_ref, cos_ref, sin_ref, e_ref, qo_ref, ko_ref, km_ref):
    blk = pl.program_id(1)
    cos = _tile_lanes(cos_ref[...], GROUP_WIDTH // LANES)
    sin = _tile_lanes(sin_ref[...], GROUP_WIDTH // LANES)
    qo_ref[0] = _rope_full(q_ref[0], cos, sin)
    k = _rope_full(k_ref[0], cos, sin)
    km_ref[0, 0] = jnp.mean(k, axis=0, keepdims=True)
    lane = lax.broadcasted_iota(jnp.int32, ko_ref.shape[1:], 1) % HEAD_LANES
    own_lane = jnp.where(lane == HEAD_DIM + blk, 1.0, 0.0)
    ko_ref[0] = (jnp.dot(k.astype(BF16), e_ref[...], preferred_element_type=F32) + own_lane).astype(BF16)


def moba_prep(q, k, *, name="moba_prep"):
    b, s, _ = q.shape
    tm = MOBA_BLOCK
    nb = s // tm
    assert nb <= MOBA_NB_PAD
    inv_freq = ROPE_THETA ** (-jnp.arange(0, HEAD_DIM, 2, dtype=F32) / HEAD_DIM)
    ang = jnp.arange(s, dtype=F32)[:, None] * inv_freq[None, :]
    cos, sin = _tile_lanes(jnp.cos(ang), LANES // (HEAD_DIM // 2)), _tile_lanes(jnp.sin(ang), LANES // (HEAD_DIM // 2))
    width = N_HEADS * HEAD_LANES
    blk = lambda n: pl.BlockSpec((1, tm, n), lambda bi, i: (bi, i, 0))
    tab = pl.BlockSpec((tm, LANES), lambda bi, i: (i, 0))
    e = _expand_matrix()
    return pl.pallas_call(
        _moba_prep_kernel,
        grid=(b, nb),
        in_specs=[blk(GROUP_WIDTH), blk(GROUP_WIDTH), tab, tab, pl.BlockSpec(e.shape, lambda bi, i: (0, 0))],
        out_specs=[blk(GROUP_WIDTH), blk(width),
                   pl.BlockSpec((1, 1, 1, GROUP_WIDTH), lambda bi, i: (bi, i, 0, 0))],
        out_shape=[jax.ShapeDtypeStruct((b, s, GROUP_WIDTH), F32), jax.ShapeDtypeStruct((b, s, width), BF16),
                   jax.ShapeDtypeStruct((b, nb, 1, GROUP_WIDTH), F32)],
        compiler_params=_cparams("parallel", "parallel"),
        name=name,
    )(q, k, cos, sin, e)


def _moba_gate_kernel(scale, q_ref, km_ref, e_ref, pm_ref, qo_ref):
    own = pl.program_id(1)
    q = q_ref[0]
    gate_t = _dot_general(km_ref[0], q, _NT, 6)
    nbp = MOBA_NB_PAD
    j = lax.broadcasted_iota(jnp.int32, (nbp, q.shape[0]), 0)
    valid = j < own
    lowest = -3.0e38
    bias_t = []
    for h in range(N_HEADS):
        g = jnp.where(valid, gate_t[h * nbp:(h + 1) * nbp, :], lowest)
        rest = g
        for _ in range(MOBA_TOPK - 1):
            rest = jnp.where(rest >= jnp.max(rest, axis=0, keepdims=True), lowest, rest)
        kth = jnp.max(rest, axis=0, keepdims=True)
        keep = (valid & (g >= kth)) | (j == own)
        bias_t.append(jnp.where(keep, 0.0, NEG_BIG))
    bias = jnp.concatenate(bias_t, axis=0).T
    out = (jnp.dot((q * scale).astype(BF16), e_ref[...], preferred_element_type=F32)
           + jnp.dot(bias.astype(BF16), pm_ref[...], preferred_element_type=F32))
    qo_ref[0] = out.astype(BF16)


def moba_gate(q_rope, k_mean, *, name="moba_gate"):
    b, s, _ = q_rope.shape
    tm = MOBA_BLOCK
    nb = s // tm
    nbp = MOBA_NB_PAD
    width = N_HEADS * HEAD_LANES
    km = k_mean.reshape(b, nb, N_HEADS, HEAD_DIM).transpose(0, 2, 1, 3)
    km = jnp.pad(km, ((0, 0), (0, 0), (0, nbp - nb), (0, 0)))
    eye = jnp.eye(N_HEADS, dtype=F32)
    km_bd = (km[:, :, :, None, :] * eye[None, :, None, :, None]).reshape(b, N_HEADS * nbp, GROUP_WIDTH)
    pm = np.zeros((N_HEADS * nbp, width), np.float32)
    for h in range(N_HEADS):
        pm[h * nbp + np.arange(nbp), h * HEAD_LANES + HEAD_DIM + np.arange(nbp)] = 1.0
    e, pm = _expand_matrix(), jnp.asarray(pm, BF16)
    return pl.pallas_call(
        functools.partial(_moba_gate_kernel, float(HEAD_DIM) ** -0.5 * LOG2E),
        grid=(b, nb),
        in_specs=[pl.BlockSpec((1, tm, GROUP_WIDTH), lambda bi, i: (bi, i, 0)),
                  pl.BlockSpec((1, N_HEADS * nbp, GROUP_WIDTH), lambda bi, i: (bi, 0, 0)),
                  pl.BlockSpec(e.shape, lambda bi, i: (0, 0)), pl.BlockSpec(pm.shape, lambda bi, i: (0, 0))],
        out_specs=pl.BlockSpec((1, tm, width), lambda bi, i: (bi, i, 0)),
        out_shape=jax.ShapeDtypeStruct((b, s, width), BF16),
        compiler_params=_cparams("parallel", "parallel"),
        name=name,
    )(q_rope, km_bd, e, pm)


def kernel(x, norm_mix_0, w_in_0, shift_mu_0, rw_w0_0, rw_w2_0, rw_a0_0, rw_a2_0, rw_g2_0, rw_kk_0, rw_ka_0,
           rw_rk_0, rw_lnx_g_0, rw_lnx_b_0, mla_qnorm_0, mla_wuq_0, mla_kvnorm_0, mla_wukv_0, w_out_0,
           norm_ffn_0, ffn_wg_0, ffn_wu_0, ffn_wd_0, norm_mix_1, w_in_1, fox_bf_1, w_out_1, norm_ffn_1,
           router_1, moe_wg_1, moe_wu_1, moe_wd_1, final_norm):
    b, s, d = x.shape
    t = b * s
    gw = GROUP_WIDTH
    bf = lambda w: w.astype(BF16)
    x2 = x.reshape(t, d)

    mla0 = RW_COLS
    w_q, w_kv = w_in_0[:, mla0:mla0 + MLA_Q_RANK], w_in_0[:, mla0 + MLA_Q_RANK:mla0 + MLA_Q_RANK + MLA_KV_RANK]
    w_kr = w_in_0[:, mla0 + MLA_Q_RANK + MLA_KV_RANK:]
    half = MLA_ROPE // 2
    z = lambda n: jnp.zeros((d, n), F32)
    w_kpe = jnp.concatenate([z(HEAD_DIM), w_kr, z(HEAD_LANES - HEAD_DIM - MLA_ROPE)], axis=1)
    w_kpe_rot = jnp.concatenate([z(HEAD_DIM), -w_kr[:, half:], w_kr[:, :half],
                                 z(HEAD_LANES - HEAD_DIM - MLA_ROPE)], axis=1)
    w_mla = jnp.concatenate([w_q, w_kv, w_kpe, w_kpe_rot], axis=1)
    rw, mla = norm_proj(x2, norm_mix_0, [bf(w_in_0[:, :RW_COLS]), bf(w_mla)], [F32, F32], name="in_proj_0")
    y_a = rwkv7(rw.reshape(b, s, RW_COLS), shift_mu_0, rw_w0_0, rw_w2_0, rw_a0_0, rw_a2_0, rw_g2_0,
                rw_kk_0, rw_ka_0, rw_rk_0, rw_lnx_g_0, rw_lnx_b_0)
    q, k, v = mla_prep(mla, mla_qnorm_0, mla_kvnorm_0, mla_wuq_0, mla_wukv_0, s)
    y_b = flash_attention(q.reshape(b, s, -1), k.reshape(b, s, -1), v.reshape(b, s, -1), name="flash_mla")
    h = out_proj(y_a.reshape(t, gw), y_b.reshape(t, gw), bf(w_out_0[:gw]), bf(w_out_0[gw:]), x2, name="out_proj_0")
    h = ffn(h, norm_ffn_0, bf(ffn_wg_0), bf(ffn_wu_0), bf(ffn_wd_0))

    c0 = 3 * gw
    w_f = jnp.pad(w_in_1[:, c0:c0 + N_HEADS], ((0, 0), (0, LANES - N_HEADS)))
    c1 = c0 + N_HEADS
    cols = [w_in_1[:, 0:gw], w_in_1[:, gw:2 * gw], w_in_1[:, 2 * gw:3 * gw], w_f,
            w_in_1[:, c1:c1 + gw], w_in_1[:, c1 + gw:c1 + 2 * gw], w_in_1[:, c1 + 2 * gw:]]
    fq, fk, fv, ff, mq, mk, mv = norm_proj(h, norm_mix_1, [bf(w) for w in cols],
                                           [F32, F32, BF16, F32, F32, F32, BF16], name="in_proj_1")
    r3 = lambda a: a.reshape(b, s, -1)
    fqa, fka = fox_prep(r3(fq), r3(fk), r3(ff), fox_bf_1)
    y_c = flash_attention(fqa, fka, r3(fv), name="flash_fox")
    mq_rope, mka, k_mean = moba_prep(r3(mq), r3(mk))
    mqa = moba_gate(mq_rope, k_mean)
    y_d = flash_attention(mqa, mka, r3(mv), name="flash_moba")
    h = out_proj(y_c.reshape(t, gw), y_d.reshape(t, gw), bf(w_out_1[:gw]), bf(w_out_1[gw:]), h, name="out_proj_1")

    router_pad = jnp.pad(router_1, ((0, 0), (0, LANES - N_EXPERTS)))
    out = moe(h, norm_ffn_1, router_pad, bf(moe_wg_1), bf(moe_wu_1), bf(moe_wd_1), final_norm)
    return out.reshape(b, s, d)
```

```python
import functools

import numpy as np
import jax
import jax.numpy as jnp
from jax import lax
from jax.experimental import pallas as pl
from jax.experimental.pallas import tpu as pltpu

F32 = jnp.float32
BF16 = jnp.bfloat16

HEAD_DIM = 64
N_HEADS = 8
GROUP_WIDTH = N_HEADS * HEAD_DIM
HEAD_LANES = 128
MLA_ROPE = 32
MLA_Q_RANK = 256
MLA_KV_RANK = 128
DECAY_LORA = 64
AAA_LORA = 64
GATE_LORA = 128
RW_COLS = 3 * GROUP_WIDTH + DECAY_LORA + AAA_LORA + GATE_LORA
RW_GN_EPS = 64e-5
MOBA_BLOCK = 256
MOBA_TOPK = 3
ROPE_THETA = 10000.0
NORM_EPS = 1e-6
N_EXPERTS = 8
NEG_BIG = -1e30

LANES = 128
VMEM_LIMIT_BYTES = 56 * 1024 * 1024

RW_CHUNK = 64
RW_GROUP_HEADS = 4
RW_GROUP_LANES = RW_GROUP_HEADS * HEAD_DIM


def _cparams(*semantics):
    return pltpu.CompilerParams(dimension_semantics=semantics, vmem_limit_bytes=VMEM_LIMIT_BYTES)


def _dot(a, b, prec=1):
    if prec == 6:
        return jnp.dot(a.astype(F32), b.astype(F32), preferred_element_type=F32,
                       precision=lax.Precision.HIGHEST)
    if prec == 1:
        return jnp.dot(a.astype(BF16), b.astype(BF16), preferred_element_type=F32)
    ah, al = _split(a)
    if b.dtype == BF16:
        return jnp.dot(ah, b, preferred_element_type=F32) + jnp.dot(al, b, preferred_element_type=F32)
    bh, bl = _split(b)
    return (jnp.dot(ah, bh, preferred_element_type=F32) + jnp.dot(ah, bl, preferred_element_type=F32)
            + jnp.dot(al, bh, preferred_element_type=F32))


def _split(x):
    hi = x.astype(BF16)
    return hi, (x - hi.astype(F32)).astype(BF16)


_NT = (((1,), (1,)), ((), ()))
_TN = (((0,), (0,)), ((), ()))


def _dot_general(a, b, dims, prec=1):
    if prec == 6:
        return lax.dot_general(a.astype(F32), b.astype(F32), dims, preferred_element_type=F32,
                               precision=lax.Precision.HIGHEST)
    if prec == 1:
        return lax.dot_general(a.astype(BF16), b.astype(BF16), dims, preferred_element_type=F32)
    ah, al = _split(a)
    bh, bl = _split(b)
    return (lax.dot_general(ah, bh, dims, preferred_element_type=F32)
            + lax.dot_general(ah, bl, dims, preferred_element_type=F32)
            + lax.dot_general(al, bh, dims, preferred_element_type=F32))


def _rms(x, g):
    return x * lax.rsqrt(jnp.mean(x * x, axis=-1, keepdims=True) + NORM_EPS) * g


def _softplus(x):
    return jnp.maximum(x, 0.0) + jnp.log(1.0 + jnp.exp(-jnp.abs(x)))


def _sigmoid(x):
    return 1.0 / (1.0 + jnp.exp(-x))


def _row_spec(tm, n):
    return pl.BlockSpec((tm, n), lambda i: (i, 0))


def _full_spec(shape):
    nd = len(shape)
    return pl.BlockSpec(shape, lambda *_: (0,) * nd)


def _norm_proj_kernel(n_out, x_ref, g_ref, *refs):
    w_refs, o_refs = refs[:n_out], refs[n_out:]
    xn = _rms(x_ref[...], g_ref[...]).astype(BF16)
    for w_ref, o_ref in zip(w_refs, o_refs):
        o_ref[...] = jnp.dot(xn, w_ref[...], preferred_element_type=F32).astype(o_ref.dtype)


def norm_proj(x, g, weights, dtypes, *, tm=256, name="norm_proj"):
    t, d = x.shape
    n_out = len(weights)
    return pl.pallas_call(
        functools.partial(_norm_proj_kernel, n_out),
        grid=(t // tm,),
        in_specs=[_row_spec(tm, d), _full_spec((1, d))] + [_full_spec(w.shape) for w in weights],
        out_specs=[_row_spec(tm, w.shape[1]) for w in weights],
        out_shape=[jax.ShapeDtypeStruct((t, w.shape[1]), dt) for w, dt in zip(weights, dtypes)],
        compiler_params=_cparams("parallel"),
        name=name,
    )(x, g.reshape(1, d), *weights)


def _out_proj_kernel(y1_ref, y2_ref, w1_ref, w2_ref, r_ref, o_ref):
    o_ref[...] = (r_ref[...] + jnp.dot(y1_ref[...], w1_ref[...], preferred_element_type=F32)
                  + jnp.dot(y2_ref[...], w2_ref[...], preferred_element_type=F32))


def out_proj(y1, y2, w1, w2, resid, *, tm=512, name="out_proj"):
    t, d = resid.shape
    return pl.pallas_call(
        _out_proj_kernel,
        grid=(t // tm,),
        in_specs=[_row_spec(tm, y1.shape[1]), _row_spec(tm, y2.shape[1]),
                  _full_spec(w1.shape), _full_spec(w2.shape), _row_spec(tm, d)],
        out_specs=_row_spec(tm, d),
        out_shape=jax.ShapeDtypeStruct((t, d), F32),
        compiler_params=_cparams("parallel"),
        name=name,
    )(y1, y2, w1, w2, resid)


def _ffn_kernel(h_ref, g_ref, wg_ref, wu_ref, wd_ref, o_ref, xn_ref, acc_ref):
    f = pl.program_id(1)

    @pl.when(f == 0)
    def _():
        xn_ref[...] = _rms(h_ref[...], g_ref[...]).astype(BF16)
        acc_ref[...] = jnp.zeros_like(acc_ref)

    xn = xn_ref[...]
    gate = jnp.dot(xn, wg_ref[...], preferred_element_type=F32)
    up = jnp.dot(xn, wu_ref[...], preferred_element_type=F32)
    act = (gate * _sigmoid(gate) * up).astype(BF16)
    acc_ref[...] += jnp.dot(act, wd_ref[...], preferred_element_type=F32)

    @pl.when(f == pl.num_programs(1) - 1)
    def _():
        o_ref[...] = h_ref[...] + acc_ref[...]


def ffn(h, g, wg, wu, wd, *, tm=1024, tf=256, name="ffn"):
    t, d = h.shape
    dff = wg.shape[1]
    return pl.pallas_call(
        _ffn_kernel,
        grid=(t // tm, dff // tf),
        in_specs=[pl.BlockSpec((tm, d), lambda i, f: (i, 0)),
                  pl.BlockSpec((1, d), lambda i, f: (0, 0)),
                  pl.BlockSpec((d, tf), lambda i, f: (0, f)),
                  pl.BlockSpec((d, tf), lambda i, f: (0, f)),
                  pl.BlockSpec((tf, d), lambda i, f: (f, 0))],
        out_specs=pl.BlockSpec((tm, d), lambda i, f: (i, 0)),
        out_shape=jax.ShapeDtypeStruct((t, d), F32),
        scratch_shapes=[pltpu.VMEM((tm, d), BF16), pltpu.VMEM((tm, d), F32)],
        compiler_params=_cparams("parallel", "arbitrary"),
        name=name,
    )(h, g.reshape(1, d), wg, wu, wd)


ROUTE_I1, ROUTE_I2, ROUTE_G1, ROUTE_G2, ROUTE_R1, ROUTE_R2 = range(6)


def _moe_route_kernel(h_ref, g_ref, router_ref, route_ref, counts_ref, carry_ref):
    @pl.when(pl.program_id(0) == 0)
    def _():
        carry_ref[...] = jnp.zeros_like(carry_ref)

    tm = h_ref.shape[0]
    lane = lax.broadcasted_iota(jnp.int32, (tm, LANES), 1)
    t = _rms(h_ref[...], g_ref[...])
    logits = _dot(t, router_ref[...], prec=6)
    logits = jnp.where(lane < N_EXPERTS, logits, NEG_BIG)
    m1 = jnp.max(logits, axis=-1, keepdims=True)
    i1 = jnp.min(jnp.where(logits == m1, lane, LANES), axis=-1, keepdims=True)
    rest = jnp.where(lane == i1, NEG_BIG, logits)
    m2 = jnp.max(rest, axis=-1, keepdims=True)
    i2 = jnp.min(jnp.where(rest == m2, lane, LANES), axis=-1, keepdims=True)
    ex = jnp.exp(m2 - m1)
    g1 = 1.0 / (1.0 + ex)
    sel = jnp.where((lane == i1) | (lane == i2), 1.0, 0.0)
    ti = lax.broadcasted_iota(jnp.int32, (tm, tm), 0)
    si = lax.broadcasted_iota(jnp.int32, (tm, tm), 1)
    before = jnp.where(si < ti, 1.0, 0.0).astype(BF16)
    rank = jnp.dot(before, sel.astype(BF16), preferred_element_type=F32) + carry_ref[0:1, :]
    carry_ref[0:1, :] = carry_ref[0:1, :] + jnp.sum(sel, axis=0, keepdims=True)
    r1 = jnp.sum(jnp.where(lane == i1, rank, 0.0), axis=-1, keepdims=True)
    r2 = jnp.sum(jnp.where(lane == i2, rank, 0.0), axis=-1, keepdims=True)
    fields = [i1.astype(F32), i2.astype(F32), g1, ex * g1, r1, r2]
    route = jnp.zeros((tm, LANES), F32)
    for idx, val in enumerate(fields):
        route = jnp.where(lane == idx, val, route)
    route_ref[...] = route
    counts_ref[...] = carry_ref[...]


def moe_route(h, g, router_pad, *, tm=512, name="moe_route"):
    t, d = h.shape
    return pl.pallas_call(
        _moe_route_kernel,
        grid=(t // tm,),
        in_specs=[_row_spec(tm, d), _full_spec((1, d)), _full_spec((d, LANES))],
        out_specs=[_row_spec(tm, LANES), _full_spec((8, LANES))],
        out_shape=[jax.ShapeDtypeStruct((t, LANES), F32), jax.ShapeDtypeStruct((8, LANES), F32)],
        scratch_shapes=[pltpu.VMEM((8, LANES), F32)],
        compiler_params=_cparams("arbitrary"),
        name=name,
    )(h, g.reshape(1, d), router_pad)


def _row_copy(src_hbm, src_row, dst, dst_row, sem):
    return pltpu.make_async_copy(src_hbm.at[pl.ds(src_row, 1)], dst.at[pl.ds(dst_row, 1)], sem)


def _moe_dispatch_kernel(pos0_ref, pos1_ref, h_hbm, init_hbm, xs_hbm, sem):
    del init_hbm
    tm = pos0_ref.shape[2]
    base = pl.program_id(0) * tm

    def start(r, c):
        _row_copy(h_hbm, base + r, xs_hbm, pos0_ref[0, 0, r], sem.at[0]).start()
        _row_copy(h_hbm, base + r, xs_hbm, pos1_ref[0, 0, r], sem.at[1]).start()
        return c

    def wait(r, c):
        _row_copy(h_hbm, base + r, xs_hbm, pos0_ref[0, 0, r], sem.at[0]).wait()
        _row_copy(h_hbm, base + r, xs_hbm, pos1_ref[0, 0, r], sem.at[1]).wait()
        return c

    lax.fori_loop(0, tm, start, 0)
    lax.fori_loop(0, tm, wait, 0)


def moe_dispatch(h, pos0, pos1, n_rows, *, tm=1024, name="moe_dispatch"):
    t, d = h.shape
    idx_spec = pl.BlockSpec((1, 1, tm), lambda i: (i, 0, 0), memory_space=pltpu.SMEM)
    any_spec = pl.BlockSpec(memory_space=pl.ANY)
    return pl.pallas_call(
        _moe_dispatch_kernel,
        grid=(t // tm,),
        in_specs=[idx_spec, idx_spec, any_spec, any_spec],
        out_specs=any_spec,
        out_shape=jax.ShapeDtypeStruct((n_rows, d), F32),
        scratch_shapes=[pltpu.SemaphoreType.DMA((2,))],
        input_output_aliases={3: 0},
        compiler_params=_cparams("arbitrary"),
        name=name,
    )(pos0.reshape(t // tm, 1, tm), pos1.reshape(t // tm, 1, tm), h, jnp.zeros((n_rows, d), F32))


def _moe_expert_kernel(te_ref, tv_ref, x_ref, g_ref, wg_ref, wu_ref, wd_ref, o_ref, xn_ref, acc_ref):
    i = pl.program_id(0)
    f = pl.program_id(1)
    last = pl.num_programs(1) - 1
    live = tv_ref[i] == 1

    @pl.when(live & (f == 0))
    def _():
        xn_ref[...] = _rms(x_ref[...], g_ref[...]).astype(BF16)
        acc_ref[...] = jnp.zeros_like(acc_ref)

    @pl.when(live)
    def _():
        xn = xn_ref[...]
        gate = jnp.dot(xn, wg_ref[0], preferred_element_type=F32)
        up = jnp.dot(xn, wu_ref[0], preferred_element_type=F32)
        act = (gate * _sigmoid(gate) * up).astype(BF16)
        acc_ref[...] += jnp.dot(act, wd_ref[0], preferred_element_type=F32)

    @pl.when(live & (f == last))
    def _():
        o_ref[...] = acc_ref[...]

    @pl.when(jnp.logical_not(live) & (f == last))
    def _():
        o_ref[...] = jnp.zeros_like(o_ref)


def moe_experts(xs, g, wg, wu, wd, tile_expert, tile_live, *, tm, tf=512, name="moe_experts"):
    n_rows, d = xs.shape
    dff = wg.shape[2]
    nf = dff // tf
    fidx = lambda i, f, te, tv: f * tv[i] + (nf - 1) * (1 - tv[i])
    grid_spec = pltpu.PrefetchScalarGridSpec(
        num_scalar_prefetch=2,
        grid=(n_rows // tm, nf),
        in_specs=[pl.BlockSpec((tm, d), lambda i, f, te, tv: (i, 0)),
                  pl.BlockSpec((1, d), lambda i, f, te, tv: (0, 0)),
                  pl.BlockSpec((1, d, tf), lambda i, f, te, tv: (te[i], 0, fidx(i, f, te, tv))),
                  pl.BlockSpec((1, d, tf), lambda i, f, te, tv: (te[i], 0, fidx(i, f, te, tv))),
                  pl.BlockSpec((1, tf, d), lambda i, f, te, tv: (te[i], fidx(i, f, te, tv), 0))],
        out_specs=pl.BlockSpec((tm, d), lambda i, f, te, tv: (i, 0)),
        scratch_shapes=[pltpu.VMEM((tm, d), BF16), pltpu.VMEM((tm, d), F32)])
    return pl.pallas_call(
        _moe_expert_kernel,
        grid_spec=grid_spec,
        out_shape=jax.ShapeDtypeStruct((n_rows, d), F32),
        compiler_params=_cparams("arbitrary", "arbitrary"),
        name=name,
    )(tile_expert, tile_live, xs, g.reshape(1, d), wg, wu, wd)


def _moe_combine_kernel(pos0_ref, pos1_ref, h_ref, route_ref, fn_ref, ys_hbm, o_ref, y0_ref, y1_ref, sem):
    tm = h_ref.shape[0]

    def start(r, c):
        _row_copy(ys_hbm, pos0_ref[0, 0, r], y0_ref, r, sem.at[0]).start()
        _row_copy(ys_hbm, pos1_ref[0, 0, r], y1_ref, r, sem.at[1]).start()
        return c

    def wait(r, c):
        _row_copy(ys_hbm, pos0_ref[0, 0, r], y0_ref, r, sem.at[0]).wait()
        _row_copy(ys_hbm, pos1_ref[0, 0, r], y1_ref, r, sem.at[1]).wait()
        return c

    lax.fori_loop(0, tm, start, 0)
    lax.fori_loop(0, tm, wait, 0)
    route = route_ref[...]
    g1 = route[:, ROUTE_G1:ROUTE_G1 + 1]
    g2 = route[:, ROUTE_G2:ROUTE_G2 + 1]
    o_ref[...] = _rms(h_ref[...] + g1 * y0_ref[...] + g2 * y1_ref[...], fn_ref[...])


def moe_combine(h, route, ys, pos0, pos1, final_g, *, tm=512, name="moe_combine"):
    t, d = h.shape
    idx_spec = pl.BlockSpec((1, 1, tm), lambda i: (i, 0, 0), memory_space=pltpu.SMEM)
    return pl.pallas_call(
        _moe_combine_kernel,
        grid=(t // tm,),
        in_specs=[idx_spec, idx_spec, _row_spec(tm, d), _row_spec(tm, LANES), _full_spec((1, d)),
                  pl.BlockSpec(memory_space=pl.ANY)],
        out_specs=_row_spec(tm, d),
        out_shape=jax.ShapeDtypeStruct((t, d), F32),
        scratch_shapes=[pltpu.VMEM((tm, d), F32), pltpu.VMEM((tm, d), F32), pltpu.SemaphoreType.DMA((2,))],
        compiler_params=_cparams("arbitrary"),
        name=name,
    )(pos0.reshape(t // tm, 1, tm), pos1.reshape(t // tm, 1, tm), h, route, final_g.reshape(1, d), ys)


def moe(h, g, router_pad, wg, wu, wd, final_g, *, tm_rows=1024):
    t, d = h.shape
    n_tiles = 2 * t // tm_rows + N_EXPERTS
    route, counts = moe_route(h, g, router_pad)
    counts = counts[0, :N_EXPERTS].astype(jnp.int32)
    padded = (counts + tm_rows - 1) // tm_rows * tm_rows
    ends = jnp.cumsum(padded)
    offsets = ends - padded
    experts = jnp.arange(N_EXPERTS, dtype=jnp.int32)

    def position(idx_lane, rank_lane):
        e = route[:, idx_lane].astype(jnp.int32)
        off = jnp.sum(jnp.where(e[:, None] == experts[None, :], offsets[None, :], 0), axis=1)
        return off + route[:, rank_lane].astype(jnp.int32)

    pos0, pos1 = position(ROUTE_I1, ROUTE_R1), position(ROUTE_I2, ROUTE_R2)
    tile_start = jnp.arange(n_tiles, dtype=jnp.int32) * tm_rows
    tile_expert = jnp.minimum(jnp.sum(tile_start[:, None] >= ends[None, :], axis=1), N_EXPERTS - 1).astype(jnp.int32)
    tile_live = (tile_start < ends[-1]).astype(jnp.int32)
    xs = moe_dispatch(h, pos0, pos1, n_tiles * tm_rows)
    ys = moe_experts(xs, g, wg, wu, wd, tile_expert, tile_live, tm=tm_rows)
    return moe_combine(h, route, ys, pos0, pos1, final_g)


LOG2E = 1.4426950408889634


def _flash_kernel(blk, q_ref, k_ref, vt_ref, o_ref, acc_ref):
    i = pl.program_id(2)
    key = lax.broadcasted_iota(jnp.int32, (blk, blk), 0)
    qry = lax.broadcasted_iota(jnp.int32, (blk, blk), 1)
    causal = key <= qry

    def scores(h, kb):
        start = pl.multiple_of(kb * blk, blk)
        k = k_ref[0, pl.ds(start, blk), h * HEAD_LANES:(h + 1) * HEAD_LANES]
        q = q_ref[0, :, h * HEAD_LANES:(h + 1) * HEAD_LANES]
        return lax.dot_general(k, q, _NT, preferred_element_type=F32)

    both = [jnp.where(causal, scores(h, i), NEG_BIG) for h in range(2)]
    ms = []
    for h in range(2):
        m = jnp.max(both[h], axis=0, keepdims=True)
        p = jnp.exp2((both[h] - m).astype(BF16))
        ms.append(m)
        acc_ref[h] = jnp.dot(vt_ref[0, h, i], p, preferred_element_type=F32)

    def body(kb, carry):
        both = [scores(h, kb) for h in range(2)]
        new = []
        for h in range(2):
            m_new = jnp.maximum(carry[h], jnp.max(both[h], axis=0, keepdims=True))
            alpha = jnp.exp2(carry[h] - m_new)
            p = jnp.exp2((both[h] - m_new).astype(BF16))
            new.append(m_new)
            acc_ref[h] = alpha * acc_ref[h] + jnp.dot(vt_ref[0, h, kb], p, preferred_element_type=F32)
        return tuple(new)

    lax.fori_loop(0, i, body, tuple(ms))
    outs = []
    for h in range(2):
        acc = acc_ref[h]
        outs.append(acc[:HEAD_DIM] / acc[HEAD_DIM:HEAD_DIM + 1])
    o_ref[0] = jnp.concatenate(outs, axis=0).T.astype(o_ref.dtype)


FLASH_V_ROWS = 80


def flash_attention(q, k, v, *, blk=512, name="flash"):
    b, s, hw = q.shape
    n_heads = hw // HEAD_LANES
    nb = s // blk
    vt = v.reshape(b, nb, blk, n_heads, HEAD_DIM).transpose(0, 3, 1, 4, 2)
    pad_rows = FLASH_V_ROWS - HEAD_DIM - 1
    vt = jnp.concatenate([vt, jnp.ones((b, n_heads, nb, 1, blk), BF16),
                          jnp.zeros((b, n_heads, nb, pad_rows, blk), BF16)], axis=3)
    return pl.pallas_call(
        functools.partial(_flash_kernel, blk),
        grid=(b, n_heads // 2, nb),
        in_specs=[pl.BlockSpec((1, blk, 2 * HEAD_LANES), lambda bi, p, i: (bi, i, p)),
                  pl.BlockSpec((1, s, 2 * HEAD_LANES), lambda bi, p, i: (bi, 0, p)),
                  pl.BlockSpec((1, 2, nb, FLASH_V_ROWS, blk), lambda bi, p, i: (bi, p, 0, 0, 0))],
        out_specs=pl.BlockSpec((1, blk, 2 * HEAD_DIM), lambda bi, p, i: (bi, i, p)),
        out_shape=jax.ShapeDtypeStruct((b, s, n_heads * HEAD_DIM), BF16),
        scratch_shapes=[pltpu.VMEM((2, FLASH_V_ROWS, blk), F32)],
        compiler_params=_cparams("parallel", "parallel", "arbitrary"),
        name=name,
    )(q, k, vt)


def _rwkv_kernel(prec, x_ref, mu_ref, w0_ref, w2_ref, a0_ref, a2_ref, g2_ref, kk_ref, ka_ref, rk_ref,
                 lng_ref, lnb_ref, o_ref, carry_ref, s_ref):
    c = RW_CHUNK
    gl = RW_GROUP_LANES
    n_groups = GROUP_WIDTH // gl

    @pl.when(pl.program_id(1) == 0)
    def _():
        carry_ref[...] = jnp.zeros_like(carry_ref)
        s_ref[...] = jnp.zeros_like(s_ref)

    x = x_ref[0]
    row = lax.broadcasted_iota(jnp.int32, x.shape, 0)
    prev = jnp.where(row == 0, carry_ref[0:1, :], pltpu.roll(x, 1, axis=0))
    carry_ref[0:1, :] = x[c - 1:c, :]
    xs = x + mu_ref[...] * (prev - x)

    gw = GROUP_WIDTH
    r, k, v = xs[:, 0:gw], xs[:, gw:2 * gw], xs[:, 2 * gw:3 * gw]
    wa = xs[:, 3 * gw:3 * gw + DECAY_LORA + AAA_LORA]
    gd = xs[:, 3 * gw + DECAY_LORA + AAA_LORA:]
    log_w = -_softplus(-(w0_ref[...] + _dot(jnp.tanh(wa), w2_ref[...], 3))) - 0.5
    lw = -jnp.exp(log_w)
    a = _sigmoid(a0_ref[...] + _dot(wa, a2_ref[...], 3))
    g = _dot(_sigmoid(gd), g2_ref[...], 3)

    brow = lax.broadcasted_iota(jnp.int32, (gl, gl), 0) // HEAD_DIM
    bcol = lax.broadcasted_iota(jnp.int32, (gl, gl), 1) // HEAD_DIM
    bmask = brow == bcol
    bones = jnp.where(bmask, 1.0, 0.0).astype(BF16)

    def head_sum(t):
        return jnp.concatenate([_dot(t[:, i * gl:(i + 1) * gl], bones, 2) for i in range(n_groups)], axis=1)

    kk = k * kk_ref[...]
    kk = kk / jnp.maximum(jnp.sqrt(head_sum(kk * kk)), 1e-12)
    k2 = k * (1.0 + (a - 1.0) * ka_ref[...])
    b = kk * a

    ti = lax.broadcasted_iota(jnp.int32, (c, c), 0)
    si = lax.broadcasted_iota(jnp.int32, (c, c), 1)
    log_p = _dot(jnp.where(si <= ti, 1.0, 0.0), lw, 6)
    log_pc = log_p[c - 1:c, :]
    inv_p = jnp.exp(-log_p)
    to_end = jnp.exp(log_pc - log_p)
    a_t = -kk * jnp.exp(log_p - lw)
    r_t = r * jnp.exp(log_p)
    b_t, k_t = b * inv_p, k2 * inv_p
    b_e, k_e = b * to_end, k2 * to_end
    p_c = jnp.exp(log_pc)

    tl = lax.broadcasted_iota(jnp.int32, (c, gl), 0)
    sl = lax.broadcasted_iota(jnp.int32, (c, gl), 1) % HEAD_DIM
    strict, incl = sl < tl, sl <= tl
    eye_l = jnp.where(sl == tl, 1.0, 0.0)
    er = lax.broadcasted_iota(jnp.int32, (gl, gl), 0)
    ec = lax.broadcasted_iota(jnp.int32, (gl, gl), 1)

    def bd(t):
        return jnp.where(bmask, jnp.concatenate([t] * RW_GROUP_HEADS, axis=0), 0.0)

    ys = []
    for gi in range(n_groups):
        sl_g = slice(gi * gl, (gi + 1) * gl)
        at_g, rt_g, bt_g, kt_g = a_t[:, sl_g], r_t[:, sl_g], b_t[:, sl_g], k_t[:, sl_g]
        be_g, ke_g, v_g = b_e[:, sl_g], k_e[:, sl_g], v[:, sl_g]
        gram = _dot_general(jnp.concatenate([at_g, rt_g], axis=0),
                            jnp.concatenate([bd(bt_g), bd(kt_g)], axis=0), _NT, prec)
        a_ab = jnp.where(strict, gram[:c, :gl], 0.0)
        a_ak = jnp.where(strict, gram[:c, gl:], 0.0)
        a_rb = jnp.where(incl, gram[c:, :gl], 0.0)
        a_rk = jnp.where(incl, gram[c:, gl:], 0.0)
        pw = _dot(a_ab, bd(a_ab), prec)
        tinv = eye_l + a_ab
        n_sq = int(np.log2(c)) - 1
        for it in range(n_sq):
            if it + 1 < n_sq:
                both = _dot(jnp.concatenate([tinv, pw], axis=0), bd(pw), prec)
                tinv, pw = tinv + both[:c], both[c:]
            else:
                tinv = tinv + _dot(tinv, bd(pw), prec)
        akv = _dot(jnp.concatenate([a_ak, a_rk], axis=0), bd(v_g), prec)
        wu = _dot(tinv, jnp.concatenate([bd(at_g), bd(akv[:c])], axis=1), prec)
        w_g, u0 = wu[:, :gl], wu[:, gl:]
        rwu = _dot(a_rb, jnp.concatenate([bd(w_g), bd(u0)], axis=1), prec)
        r_hat = rt_g + rwu[:, :gl]
        y0 = akv[c:] + rwu[:, gl:]
        m_bd = (jnp.where(bmask, _dot_general(be_g, w_g, _TN, prec), 0.0)
                + jnp.where(er == ec, p_c[:, sl_g], 0.0))
        z_bd = jnp.where(bmask, _dot_general(jnp.concatenate([ke_g, be_g], axis=0),
                                             jnp.concatenate([v_g, u0], axis=0), _TN, prec), 0.0)
        s0 = s_ref[gi]
        ys.append(_dot(r_hat, s0, 3) + y0)
        s_ref[gi] = _dot(m_bd, s0, 3) + z_bd
    y = jnp.concatenate(ys, axis=1)

    inv_n = 1.0 / HEAD_DIM
    d = y - head_sum(y) * inv_n
    yn = d * lax.rsqrt(head_sum(d * d) * inv_n + RW_GN_EPS) * lng_ref[...] + lnb_ref[...]
    bonus = head_sum(r * k2 * rk_ref[...]) * v
    o_ref[0] = ((yn + bonus) * g).astype(o_ref.dtype)


def rwkv7(rw, mu, w0, w2, a0, a2, g2, k_k, k_a, r_k, lnx_g, lnx_b, *, prec=1, name="rwkv7"):
    b, s, cols = rw.shape
    gw = GROUP_WIDTH
    zeros = jnp.zeros((DECAY_LORA, gw), F32)
    w2p = jnp.concatenate([w2, zeros], axis=0)
    a2p = jnp.concatenate([zeros, a2], axis=0)
    vec = lambda t: t.reshape(1, -1).astype(F32)
    params = [vec(mu), vec(w0), w2p, vec(a0), a2p, g2, vec(k_k), vec(k_a), vec(r_k), vec(lnx_g), vec(lnx_b)]
    return pl.pallas_call(
        functools.partial(_rwkv_kernel, prec),
        grid=(b, s // RW_CHUNK),
        in_specs=[pl.BlockSpec((1, RW_CHUNK, cols), lambda bi, ci: (bi, ci, 0))]
                 + [pl.BlockSpec(p.shape, lambda bi, ci: (0, 0)) for p in params],
        out_specs=pl.BlockSpec((1, RW_CHUNK, gw), lambda bi, ci: (bi, ci, 0)),
        out_shape=jax.ShapeDtypeStruct((b, s, gw), BF16),
        scratch_shapes=[pltpu.VMEM((8, cols), F32),
                        pltpu.VMEM((gw // RW_GROUP_LANES, RW_GROUP_LANES, RW_GROUP_LANES), F32)],
        compiler_params=_cparams("parallel", "arbitrary"),
        name=name,
    )(rw, *params)


def _tile_lanes(t, n):
    return jnp.concatenate([t] * n, axis=1)


def _expand_matrix():
    e = np.zeros((GROUP_WIDTH, N_HEADS * HEAD_LANES), np.float32)
    for h in range(N_HEADS):
        e[h * HEAD_DIM + np.arange(HEAD_DIM), h * HEAD_LANES + np.arange(HEAD_DIM)] = 1.0
    return jnp.asarray(e, BF16)


def _mla_prep_kernel(x_ref, qn_ref, kvn_ref, wq_ref, wqr_ref, wk_ref, wv_ref,
                     cq_ref, sq_ref, ck_ref, sk_ref, q_ref, k_ref, v_ref):
    x = x_ref[...]
    q_lat = x[:, :MLA_Q_RANK]
    kv_lat = x[:, MLA_Q_RANK:MLA_Q_RANK + MLA_KV_RANK]
    kpe = x[:, MLA_Q_RANK + MLA_KV_RANK:MLA_Q_RANK + MLA_KV_RANK + HEAD_LANES]
    kpe_rot = x[:, MLA_Q_RANK + MLA_KV_RANK + HEAD_LANES:]
    qn = _rms(q_lat, qn_ref[...]).astype(BF16)
    q = (jnp.dot(qn, wq_ref[...], preferred_element_type=F32) * _tile_lanes(cq_ref[...], N_HEADS)
         + jnp.dot(qn, wqr_ref[...], preferred_element_type=F32) * _tile_lanes(sq_ref[...], N_HEADS))
    q_ref[...] = q.astype(BF16)
    kn = _rms(kv_lat, kvn_ref[...]).astype(BF16)
    k_rope = kpe * ck_ref[...] + kpe_rot * sk_ref[...]
    k = jnp.dot(kn, wk_ref[...], preferred_element_type=F32) + _tile_lanes(k_rope, N_HEADS)
    k_ref[...] = k.astype(BF16)
    v_ref[...] = jnp.dot(kn, wv_ref[...], preferred_element_type=F32).astype(BF16)


def mla_prep(x, q_norm, kv_norm, w_uq, w_ukv, seq_len, *, tm=256, name="mla_prep"):
    t = x.shape[0]
    nope, rope, half = HEAD_DIM, MLA_ROPE, MLA_ROPE // 2
    wq = w_uq.reshape(MLA_Q_RANK, N_HEADS, nope + rope)
    wq_pe = wq[:, :, nope:]
    pad = lambda a: jnp.pad(a, ((0, 0), (0, 0), (0, HEAD_LANES - a.shape[2])))
    flat = lambda a: a.reshape(a.shape[0], -1).astype(BF16)
    wq_aug = flat(pad(wq))
    wq_rot = flat(pad(jnp.concatenate([jnp.zeros_like(wq[:, :, :nope]), -wq_pe[:, :, half:], wq_pe[:, :, :half]], axis=2)))
    wkv = w_ukv.reshape(MLA_KV_RANK, N_HEADS, 2 * HEAD_DIM)
    wk_aug = flat(pad(wkv[:, :, :nope]))
    wv = flat(wkv[:, :, nope:])

    scale = float(nope + rope) ** -0.5 * LOG2E
    inv_freq = ROPE_THETA ** (-jnp.arange(0, rope, 2, dtype=F32) / rope)
    ang = jnp.arange(seq_len, dtype=F32)[:, None] * inv_freq[None, :]
    cos2, sin2 = _tile_lanes(jnp.cos(ang), 2), _tile_lanes(jnp.sin(ang), 2)
    z64, z32 = jnp.zeros((seq_len, nope), F32), jnp.zeros((seq_len, HEAD_LANES - nope - rope), F32)
    cq = jnp.concatenate([jnp.full((seq_len, nope), scale, F32), cos2 * scale, z32], axis=1)
    sq = jnp.concatenate([z64, sin2 * scale, z32], axis=1)
    ck = jnp.concatenate([z64, cos2, z32], axis=1)
    sk = jnp.concatenate([z64, sin2, z32], axis=1)

    n_seq_tiles = seq_len // tm
    tab_spec = pl.BlockSpec((tm, HEAD_LANES), lambda i: (i % n_seq_tiles, 0))
    weights = [wq_aug, wq_rot, wk_aug, wv]
    return pl.pallas_call(
        _mla_prep_kernel,
        grid=(t // tm,),
        in_specs=[_row_spec(tm, x.shape[1]), _full_spec((1, MLA_Q_RANK)), _full_spec((1, MLA_KV_RANK))]
                 + [_full_spec(w.shape) for w in weights] + [tab_spec] * 4,
        out_specs=[_row_spec(tm, N_HEADS * HEAD_LANES), _row_spec(tm, N_HEADS * HEAD_LANES),
                   _row_spec(tm, GROUP_WIDTH)],
        out_shape=[jax.ShapeDtypeStruct((t, N_HEADS * HEAD_LANES), BF16),
                   jax.ShapeDtypeStruct((t, N_HEADS * HEAD_LANES), BF16),
                   jax.ShapeDtypeStruct((t, GROUP_WIDTH), BF16)],
        compiler_params=_cparams("parallel"),
        name=name,
    )(x, q_norm.reshape(1, -1), kv_norm.reshape(1, -1), *weights, cq, sq, ck, sk)


FOX_PARTS = 3


def _fox_prep_kernel(scale, q_ref, k_ref, f_ref, bf_ref, e_ref, pq_ref, pk_ref, cq_ref, ck_ref,
                     qo_ref, ko_ref, carry_ref):
    @pl.when(pl.program_id(1) == 0)
    def _():
        carry_ref[...] = jnp.zeros_like(carry_ref)

    tm = q_ref.shape[1]
    log_f = -_softplus(-(f_ref[0] + bf_ref[...]))
    ti = lax.broadcasted_iota(jnp.int32, (tm, tm), 0)
    si = lax.broadcasted_iota(jnp.int32, (tm, tm), 1)
    d = _dot(jnp.where(si <= ti, 1.0, 0.0), log_f, 6) + carry_ref[0:1, :]
    carry_ref[0:1, :] = d[tm - 1:tm, :]
    d2 = d * LOG2E
    hi = d2.astype(BF16)
    rem = d2 - hi.astype(F32)
    mid = rem.astype(BF16)
    lo = (rem - mid.astype(F32)).astype(BF16)
    parts = jnp.concatenate([hi, mid, lo], axis=1)
    q = (jnp.dot((q_ref[0] * scale).astype(BF16), e_ref[...], preferred_element_type=F32)
         + jnp.dot(parts, pq_ref[...], preferred_element_type=F32) + cq_ref[...])
    k = (jnp.dot(k_ref[0].astype(BF16), e_ref[...], preferred_element_type=F32)
         + jnp.dot(parts, pk_ref[...], preferred_element_type=F32) + ck_ref[...])
    qo_ref[0] = q.astype(BF16)
    ko_ref[0] = k.astype(BF16)


def fox_prep(q, k, f_logit, b_f, *, tm=256, name="fox_prep"):
    b, s, _ = q.shape
    width = N_HEADS * HEAD_LANES
    pq = np.zeros((FOX_PARTS * LANES, width), np.float32)
    pk = np.zeros((FOX_PARTS * LANES, width), np.float32)
    cq = np.zeros((1, width), np.float32)
    ck = np.zeros((1, width), np.float32)
    for h in range(N_HEADS):
        base = h * HEAD_LANES + HEAD_DIM
        for p in range(FOX_PARTS):
            pk[p * LANES + h, base + p] = -1.0
            pq[p * LANES + h, base + FOX_PARTS + p] = 1.0
            cq[0, base + p] = 1.0
            ck[0, base + FOX_PARTS + p] = 1.0
    consts = [_expand_matrix(), jnp.asarray(pq, BF16), jnp.asarray(pk, BF16), jnp.asarray(cq), jnp.asarray(ck)]
    bf_pad = jnp.zeros((1, LANES), F32).at[0, :N_HEADS].set(b_f.astype(F32))
    blk = lambda n: pl.BlockSpec((1, tm, n), lambda bi, i: (bi, i, 0))
    return pl.pallas_call(
        functools.partial(_fox_prep_kernel, float(HEAD_DIM) ** -0.5 * LOG2E),
        grid=(b, s // tm),
        in_specs=[blk(GROUP_WIDTH), blk(GROUP_WIDTH), blk(LANES), pl.BlockSpec((1, LANES), lambda bi, i: (0, 0))]
                 + [pl.BlockSpec(c.shape, lambda bi, i: (0, 0)) for c in consts],
        out_specs=[blk(width), blk(width)],
        out_shape=[jax.ShapeDtypeStruct((b, s, width), BF16)] * 2,
        scratch_shapes=[pltpu.VMEM((8, LANES), F32)],
        compiler_params=_cparams("parallel", "arbitrary"),
        name=name,
    )(q, k, f_logit, bf_pad, *consts)


MOBA_NB_PAD = 32


def _rope_full(x, cos, sin):
    half = HEAD_DIM // 2
    lane = lax.broadcasted_iota(jnp.int32, x.shape, 1)
    width = x.shape[1]
    rot = jnp.where(lane % HEAD_DIM < half, -pltpu.roll(x, width - half, axis=1), pltpu.roll(x, half, axis=1))
    return x * cos + rot * sin


def _moba_prep_kernel(q_ref, k_ref, cos_ref, sin_ref, e_ref, qo_ref, ko_ref, km_ref):
    blk = pl.program_id(1)
    cos = _tile_lanes(cos_ref[...], GROUP_WIDTH // LANES)
    sin = _tile_lanes(sin_ref[...], GROUP_WIDTH // LANES)
    qo_ref[0] = _rope_full(q_ref[0], cos, sin)
    k = _rope_full(k_ref[0], cos, sin)
    km_ref[0, 0] = jnp.mean(k, axis=0, keepdims=True)
    lane = lax.broadcasted_iota(jnp.int32, ko_ref.shape[1:], 1) % HEAD_LANES
    own_lane = jnp.where(lane == HEAD_DIM + blk, 1.0, 0.0)
    ko_ref[0] = (jnp.dot(k.astype(BF16), e_ref[...], preferred_element_type=F32) + own_lane).astype(BF16)


def moba_prep(q, k, *, name="moba_prep"):
    b, s, _ = q.shape
    tm = MOBA_BLOCK
    nb = s // tm
    assert nb <= MOBA_NB_PAD
    inv_freq = ROPE_THETA ** (-jnp.arange(0, HEAD_DIM, 2, dtype=F32) / HEAD_DIM)
    ang = jnp.arange(s, dtype=F32)[:, None] * inv_freq[None, :]
    cos, sin = _tile_lanes(jnp.cos(ang), LANES // (HEAD_DIM // 2)), _tile_lanes(jnp.sin(ang), LANES // (HEAD_DIM // 2))
    width = N_HEADS * HEAD_LANES
    blk = lambda n: pl.BlockSpec((1, tm, n), lambda bi, i: (bi, i, 0))
    tab = pl.BlockSpec((tm, LANES), lambda bi, i: (i, 0))
    e = _expand_matrix()
    return pl.pallas_call(
        _moba_prep_kernel,
        grid=(b, nb),
        in_specs=[blk(GROUP_WIDTH), blk(GROUP_WIDTH), tab, tab, pl.BlockSpec(e.shape, lambda bi, i: (0, 0))],
        out_specs=[blk(GROUP_WIDTH), blk(width),
                   pl.BlockSpec((1, 1, 1, GROUP_WIDTH), lambda bi, i: (bi, i, 0, 0))],
        out_shape=[jax.ShapeDtypeStruct((b, s, GROUP_WIDTH), F32), jax.ShapeDtypeStruct((b, s, width), BF16),
                   jax.ShapeDtypeStruct((b, nb, 1, GROUP_WIDTH), F32)],
        compiler_params=_cparams("parallel", "parallel"),
        name=name,
    )(q, k, cos, sin, e)


def _moba_gate_kernel(scale, q_ref, km_ref, e_ref, pm_ref, qo_ref):
    own = pl.program_id(1)
    q = q_ref[0]
    gate_t = _dot_general(km_ref[0], q, _NT, 6)
    nbp = MOBA_NB_PAD
    j = lax.broadcasted_iota(jnp.int32, (nbp, q.shape[0]), 0)
    valid = j < own
    lowest = -3.0e38
    bias_t = []
    for h in range(N_HEADS):
        g = jnp.where(valid, gate_t[h * nbp:(h + 1) * nbp, :], lowest)
        rest = g
        for _ in range(MOBA_TOPK - 1):
            rest = jnp.where(rest >= jnp.max(rest, axis=0, keepdims=True), lowest, rest)
        kth = jnp.max(rest, axis=0, keepdims=True)
        keep = (valid & (g >= kth)) | (j == own)
        bias_t.append(jnp.where(keep, 0.0, NEG_BIG))
    bias = jnp.concatenate(bias_t, axis=0).T
    out = (jnp.dot((q * scale).astype(BF16), e_ref[...], preferred_element_type=F32)
           + jnp.dot(bias.astype(BF16), pm_ref[...], preferred_element_type=F32))
    qo_ref[0] = out.astype(BF16)


def moba_gate(q_rope, k_mean, *, name="moba_gate"):
    b, s, _ = q_rope.shape
    tm = MOBA_BLOCK
    nb = s // tm
    nbp = MOBA_NB_PAD
    width = N_HEADS * HEAD_LANES
    km = k_mean.reshape(b, nb, N_HEADS, HEAD_DIM).transpose(0, 2, 1, 3)
    km = jnp.pad(km, ((0, 0), (0, 0), (0, nbp - nb), (0, 0)))
    eye = jnp.eye(N_HEADS, dtype=F32)
    km_bd = (km[:, :, :, None, :] * eye[None, :, None, :, None]).reshape(b, N_HEADS * nbp, GROUP_WIDTH)
    pm = np.zeros((N_HEADS * nbp, width), np.float32)
    for h in range(N_HEADS):
        pm[h * nbp + np.arange(nbp), h * HEAD_LANES + HEAD_DIM + np.arange(nbp)] = 1.0
    e, pm = _expand_matrix(), jnp.asarray(pm, BF16)
    return pl.pallas_call(
        functools.partial(_moba_gate_kernel, float(HEAD_DIM) ** -0.5 * LOG2E),
        grid=(b, nb),
        in_specs=[pl.BlockSpec((1, tm, GROUP_WIDTH), lambda bi, i: (bi, i, 0)),
                  pl.BlockSpec((1, N_HEADS * nbp, GROUP_WIDTH), lambda bi, i: (bi, 0, 0)),
                  pl.BlockSpec(e.shape, lambda bi, i: (0, 0)), pl.BlockSpec(pm.shape, lambda bi, i: (0, 0))],
        out_specs=pl.BlockSpec((1, tm, width), lambda bi, i: (bi, i, 0)),
        out_shape=jax.ShapeDtypeStruct((b, s, width), BF16),
        compiler_params=_cparams("parallel", "parallel"),
        name=name,
    )(q_rope, km_bd, e, pm)


def kernel(x, norm_mix_0, w_in_0, shift_mu_0, rw_w0_0, rw_w2_0, rw_a0_0, rw_a2_0, rw_g2_0, rw_kk_0, rw_ka_0,
           rw_rk_0, rw_lnx_g_0, rw_lnx_b_0, mla_qnorm_0, mla_wuq_0, mla_kvnorm_0, mla_wukv_0, w_out_0,
           norm_ffn_0, ffn_wg_0, ffn_wu_0, ffn_wd_0, norm_mix_1, w_in_1, fox_bf_1, w_out_1, norm_ffn_1,
           router_1, moe_wg_1, moe_wu_1, moe_wd_1, final_norm):
    b, s, d = x.shape
    t = b * s
    gw = GROUP_WIDTH
    bf = lambda w: w.astype(BF16)
    x2 = x.reshape(t, d)

    mla0 = RW_COLS
    w_q, w_kv = w_in_0[:, mla0:mla0 + MLA_Q_RANK], w_in_0[:, mla0 + MLA_Q_RANK:mla0 + MLA_Q_RANK + MLA_KV_RANK]
    w_kr = w_in_0[:, mla0 + MLA_Q_RANK + MLA_KV_RANK:]
    half = MLA_ROPE // 2
    z = lambda n: jnp.zeros((d, n), F32)
    w_kpe = jnp.concatenate([z(HEAD_DIM), w_kr, z(HEAD_LANES - HEAD_DIM - MLA_ROPE)], axis=1)
    w_kpe_rot = jnp.concatenate([z(HEAD_DIM), -w_kr[:, half:], w_kr[:, :half],
                                 z(HEAD_LANES - HEAD_DIM - MLA_ROPE)], axis=1)
    w_mla = jnp.concatenate([w_q, w_kv, w_kpe, w_kpe_rot], axis=1)
    rw, mla = norm_proj(x2, norm_mix_0, [bf(w_in_0[:, :RW_COLS]), bf(w_mla)], [F32, F32], name="in_proj_0")
    y_a = rwkv7(rw.reshape(b, s, RW_COLS), shift_mu_0, rw_w0_0, rw_w2_0, rw_a0_0, rw_a2_0, rw_g2_0,
                rw_kk_0, rw_ka_0, rw_rk_0, rw_lnx_g_0, rw_lnx_b_0)
    q, k, v = mla_prep(mla, mla_qnorm_0, mla_kvnorm_0, mla_wuq_0, mla_wukv_0, s)
    y_b = flash_attention(q.reshape(b, s, -1), k.reshape(b, s, -1), v.reshape(b, s, -1), name="flash_mla")
    h = out_proj(y_a.reshape(t, gw), y_b.reshape(t, gw), bf(w_out_0[:gw]), bf(w_out_0[gw:]), x2, name="out_proj_0")
    h = ffn(h, norm_ffn_0, bf(ffn_wg_0), bf(ffn_wu_0), bf(ffn_wd_0))

    c0 = 3 * gw
    w_f = jnp.pad(w_in_1[:, c0:c0 + N_HEADS], ((0, 0), (0, LANES - N_HEADS)))
    c1 = c0 + N_HEADS
    cols = [w_in_1[:, 0:gw], w_in_1[:, gw:2 * gw], w_in_1[:, 2 * gw:3 * gw], w_f,
            w_in_1[:, c1:c1 + gw], w_in_1[:, c1 + gw:c1 + 2 * gw], w_in_1[:, c1 + 2 * gw:]]
    fq, fk, fv, ff, mq, mk, mv = norm_proj(h, norm_mix_1, [bf(w) for w in cols],
                                           [F32, F32, BF16, F32, F32, F32, BF16], name="in_proj_1")
    r3 = lambda a: a.reshape(b, s, -1)
    fqa, fka = fox_prep(r3(fq), r3(fk), r3(ff), fox_bf_1)
    y_c = flash_attention(fqa, fka, r3(fv), name="flash_fox")
    mq_rope, mka, k_mean = moba_prep(r3(mq), r3(mk))
    mqa = moba_gate(mq_rope, k_mean)
    y_d = flash_attention(mqa, mka, r3(mv), name="flash_moba")
    h = out_proj(y_c.reshape(t, gw), y_d.reshape(t, gw), bf(w_out_1[:gw]), bf(w_out_1[gw:]), h, name="out_proj_1")

    router_pad = jnp.pad(router_1, ((0, 0), (0, LANES - N_EXPERTS)))
    out = moe(h, norm_ffn_1, router_pad, bf(moe_wg_1), bf(moe_wu_1), bf(moe_wd_1), final_norm)
    return out.reshape(b, s, d)
```

```python
import functools

import numpy as np
import jax
import jax.numpy as jnp
from jax import lax
from jax.experimental import pallas as pl
from jax.experimental.pallas import tpu as pltpu

F32 = jnp.float32
BF16 = jnp.bfloat16

HEAD_DIM = 64
N_HEADS = 8
GROUP_WIDTH = N_HEADS * HEAD_DIM
HEAD_LANES = 128
MLA_ROPE = 32
MLA_Q_RANK = 256
MLA_KV_RANK = 128
DECAY_LORA = 64
AAA_LORA = 64
GATE_LORA = 128
RW_COLS = 3 * GROUP_WIDTH + DECAY_LORA + AAA_LORA + GATE_LORA
RW_GN_EPS = 64e-5
MOBA_BLOCK = 256
MOBA_TOPK = 3
ROPE_THETA = 10000.0
NORM_EPS = 1e-6
N_EXPERTS = 8
NEG_BIG = -1e30

LANES = 128
VMEM_LIMIT_BYTES = 56 * 1024 * 1024

RW_CHUNK = 64
RW_GROUP_HEADS = 4
RW_GROUP_LANES = RW_GROUP_HEADS * HEAD_DIM


def _cparams(*semantics):
    return pltpu.CompilerParams(dimension_semantics=semantics, vmem_limit_bytes=VMEM_LIMIT_BYTES)


def _dot(a, b, prec=1):
    if prec == 6:
        return jnp.dot(a.astype(F32), b.astype(F32), preferred_element_type=F32,
                       precision=lax.Precision.HIGHEST)
    if prec == 1:
        return jnp.dot(a.astype(BF16), b.astype(BF16), preferred_element_type=F32)
    ah, al = _split(a)
    if b.dtype == BF16:
        return jnp.dot(ah, b, preferred_element_type=F32) + jnp.dot(al, b, preferred_element_type=F32)
    bh, bl = _split(b)
    return (jnp.dot(ah, bh, preferred_element_type=F32) + jnp.dot(ah, bl, preferred_element_type=F32)
            + jnp.dot(al, bh, preferred_element_type=F32))


def _split(x):
    hi = x.astype(BF16)
    return hi, (x - hi.astype(F32)).astype(BF16)


_NT = (((1,), (1,)), ((), ()))
_TN = (((0,), (0,)), ((), ()))


def _dot_general(a, b, dims, prec=1):
    if prec == 6:
        return lax.dot_general(a.astype(F32), b.astype(F32), dims, preferred_element_type=F32,
                               precision=lax.Precision.HIGHEST)
    if prec == 1:
        return lax.dot_general(a.astype(BF16), b.astype(BF16), dims, preferred_element_type=F32)
    ah, al = _split(a)
    bh, bl = _split(b)
    return (lax.dot_general(ah, bh, dims, preferred_element_type=F32)
            + lax.dot_general(ah, bl, dims, preferred_element_type=F32)
            + lax.dot_general(al, bh, dims, preferred_element_type=F32))


def _rms(x, g):
    return x * lax.rsqrt(jnp.mean(x * x, axis=-1, keepdims=True) + NORM_EPS) * g


def _softplus(x):
    return jnp.maximum(x, 0.0) + jnp.log(1.0 + jnp.exp(-jnp.abs(x)))


def _sigmoid(x):
    return 1.0 / (1.0 + jnp.exp(-x))


def _row_spec(tm, n):
    return pl.BlockSpec((tm, n), lambda i: (i, 0))


def _full_spec(shape):
    nd = len(shape)
    return pl.BlockSpec(shape, lambda *_: (0,) * nd)


def _norm_proj_kernel(n_out, x_ref, g_ref, *refs):
    w_refs, o_refs = refs[:n_out], refs[n_out:]
    xn = _rms(x_ref[...], g_ref[...]).astype(BF16)
    for w_ref, o_ref in zip(w_refs, o_refs):
        o_ref[...] = jnp.dot(xn, w_ref[...], preferred_element_type=F32).astype(o_ref.dtype)


def norm_proj(x, g, weights, dtypes, *, tm=256, name="norm_proj"):
    t, d = x.shape
    n_out = len(weights)
    return pl.pallas_call(
        functools.partial(_norm_proj_kernel, n_out),
        grid=(t // tm,),
        in_specs=[_row_spec(tm, d), _full_spec((1, d))] + [_full_spec(w.shape) for w in weights],
        out_specs=[_row_spec(tm, w.shape[1]) for w in weights],
        out_shape=[jax.ShapeDtypeStruct((t, w.shape[1]), dt) for w, dt in zip(weights, dtypes)],
        compiler_params=_cparams("parallel"),
        name=name,
    )(x, g.reshape(1, d), *weights)


def _out_proj_kernel(y1_ref, y2_ref, w1_ref, w2_ref, r_ref, o_ref):
    o_ref[...] = (r_ref[...] + jnp.dot(y1_ref[...], w1_ref[...], preferred_element_type=F32)
                  + jnp.dot(y2_ref[...], w2_ref[...], preferred_element_type=F32))


def out_proj(y1, y2, w1, w2, resid, *, tm=512, name="out_proj"):
    t, d = resid.shape
    return pl.pallas_call(
        _out_proj_kernel,
        grid=(t // tm,),
        in_specs=[_row_spec(tm, y1.shape[1]), _row_spec(tm, y2.shape[1]),
                  _full_spec(w1.shape), _full_spec(w2.shape), _row_spec(tm, d)],
        out_specs=_row_spec(tm, d),
        out_shape=jax.ShapeDtypeStruct((t, d), F32),
        compiler_params=_cparams("parallel"),
        name=name,
    )(y1, y2, w1, w2, resid)


def _ffn_kernel(h_ref, g_ref, wg_ref, wu_ref, wd_ref, o_ref, xn_ref, acc_ref):
    f = pl.program_id(1)

    @pl.when(f == 0)
    def _():
        xn_ref[...] = _rms(h_ref[...], g_ref[...]).astype(BF16)
        acc_ref[...] = jnp.zeros_like(acc_ref)

    xn = xn_ref[...]
    gate = jnp.dot(xn, wg_ref[...], preferred_element_type=F32)
    up = jnp.dot(xn, wu_ref[...], preferred_element_type=F32)
    act = (gate * _sigmoid(gate) * up).astype(BF16)
    acc_ref[...] += jnp.dot(act, wd_ref[...], preferred_element_type=F32)

    @pl.when(f == pl.num_programs(1) - 1)
    def _():
        o_ref[...] = h_ref[...] + acc_ref[...]


def ffn(h, g, wg, wu, wd, *, tm=1024, tf=256, name="ffn"):
    t, d = h.shape
    dff = wg.shape[1]
    return pl.pallas_call(
        _ffn_kernel,
        grid=(t // tm, dff // tf),
        in_specs=[pl.BlockSpec((tm, d), lambda i, f: (i, 0)),
                  pl.BlockSpec((1, d), lambda i, f: (0, 0)),
                  pl.BlockSpec((d, tf), lambda i, f: (0, f)),
                  pl.BlockSpec((d, tf), lambda i, f: (0, f)),
                  pl.BlockSpec((tf, d), lambda i, f: (f, 0))],
        out_specs=pl.BlockSpec((tm, d), lambda i, f: (i, 0)),
        out_shape=jax.ShapeDtypeStruct((t, d), F32),
        scratch_shapes=[pltpu.VMEM((tm, d), BF16), pltpu.VMEM((tm, d), F32)],
        compiler_params=_cparams("parallel", "arbitrary"),
        name=name,
    )(h, g.reshape(1, d), wg, wu, wd)


ROUTE_I1, ROUTE_I2, ROUTE_G1, ROUTE_G2, ROUTE_R1, ROUTE_R2 = range(6)


def _moe_route_kernel(h_ref, g_ref, router_ref, route_ref, counts_ref, carry_ref):
    @pl.when(pl.program_id(0) == 0)
    def _():
        carry_ref[...] = jnp.zeros_like(carry_ref)

    tm = h_ref.shape[0]
    lane = lax.broadcasted_iota(jnp.int32, (tm, LANES), 1)
    t = _rms(h_ref[...], g_ref[...])
    logits = _dot(t, router_ref[...], prec=6)
    logits = jnp.where(lane < N_EXPERTS, logits, NEG_BIG)
    m1 = jnp.max(logits, axis=-1, keepdims=True)
    i1 = jnp.min(jnp.where(logits == m1, lane, LANES), axis=-1, keepdims=True)
    rest = jnp.where(lane == i1, NEG_BIG, logits)
    m2 = jnp.max(rest, axis=-1, keepdims=True)
    i2 = jnp.min(jnp.where(rest == m2, lane, LANES), axis=-1, keepdims=True)
    ex = jnp.exp(m2 - m1)
    g1 = 1.0 / (1.0 + ex)
    sel = jnp.where((lane == i1) | (lane == i2), 1.0, 0.0)
    ti = lax.broadcasted_iota(jnp.int32, (tm, tm), 0)
    si = lax.broadcasted_iota(jnp.int32, (tm, tm), 1)
    before = jnp.where(si < ti, 1.0, 0.0).astype(BF16)
    rank = jnp.dot(before, sel.astype(BF16), preferred_element_type=F32) + carry_ref[0:1, :]
    carry_ref[0:1, :] = carry_ref[0:1, :] + jnp.sum(sel, axis=0, keepdims=True)
    r1 = jnp.sum(jnp.where(lane == i1, rank, 0.0), axis=-1, keepdims=True)
    r2 = jnp.sum(jnp.where(lane == i2, rank, 0.0), axis=-1, keepdims=True)
    fields = [i1.astype(F32), i2.astype(F32), g1, ex * g1, r1, r2]
    route = jnp.zeros((tm, LANES), F32)
    for idx, val in enumerate(fields):
        route = jnp.where(lane == idx, val, route)
    route_ref[...] = route
    counts_ref[...] = carry_ref[...]


def moe_route(h, g, router_pad, *, tm=512, name="moe_route"):
    t, d = h.shape
    return pl.pallas_call(
        _moe_route_kernel,
        grid=(t // tm,),
        in_specs=[_row_spec(tm, d), _full_spec((1, d)), _full_spec((d, LANES))],
        out_specs=[_row_spec(tm, LANES), _full_spec((8, LANES))],
        out_shape=[jax.ShapeDtypeStruct((t, LANES), F32), jax.ShapeDtypeStruct((8, LANES), F32)],
        scratch_shapes=[pltpu.VMEM((8, LANES), F32)],
        compiler_params=_cparams("arbitrary"),
        name=name,
    )(h, g.reshape(1, d), router_pad)


def _row_copy(src_hbm, src_row, dst, dst_row, sem):
    return pltpu.make_async_copy(src_hbm.at[pl.ds(src_row, 1)], dst.at[pl.ds(dst_row, 1)], sem)


def _moe_dispatch_kernel(pos0_ref, pos1_ref, h_ref, init_hbm, xs_hbm, sem):
    del init_hbm
    tm = h_ref.shape[0]

    def start(r, c):
        _row_copy(h_ref, r, xs_hbm, pos0_ref[0, 0, r], sem.at[0]).start()
        _row_copy(h_ref, r, xs_hbm, pos1_ref[0, 0, r], sem.at[1]).start()
        return c

    def wait(r, c):
        _row_copy(h_ref, r, xs_hbm, pos0_ref[0, 0, r], sem.at[0]).wait()
        _row_copy(h_ref, r, xs_hbm, pos1_ref[0, 0, r], sem.at[1]).wait()
        return c

    lax.fori_loop(0, tm, start, 0, unroll=8)
    lax.fori_loop(0, tm, wait, 0, unroll=8)


def moe_dispatch(h, pos0, pos1, n_rows, *, tm=1024, name="moe_dispatch"):
    t, d = h.shape
    idx_spec = pl.BlockSpec((1, 1, tm), lambda i: (i, 0, 0), memory_space=pltpu.SMEM)
    any_spec = pl.BlockSpec(memory_space=pl.ANY)
    return pl.pallas_call(
        _moe_dispatch_kernel,
        grid=(t // tm,),
        in_specs=[idx_spec, idx_spec, _row_spec(tm, d), any_spec],
        out_specs=any_spec,
        out_shape=jax.ShapeDtypeStruct((n_rows, d), F32),
        scratch_shapes=[pltpu.SemaphoreType.DMA((2,))],
        input_output_aliases={3: 0},
        compiler_params=_cparams("arbitrary"),
        name=name,
    )(pos0.reshape(t // tm, 1, tm), pos1.reshape(t // tm, 1, tm), h, jnp.zeros((n_rows, d), F32))


def _moe_expert_kernel(te_ref, tv_ref, x_ref, g_ref, wg_ref, wu_ref, wd_ref, o_ref, xn_ref, acc_ref):
    i = pl.program_id(0)
    f = pl.program_id(1)
    last = pl.num_programs(1) - 1
    live = tv_ref[i] == 1

    @pl.when(live & (f == 0))
    def _():
        xn_ref[...] = _rms(x_ref[...], g_ref[...]).astype(BF16)
        acc_ref[...] = jnp.zeros_like(acc_ref)

    @pl.when(live)
    def _():
        xn = xn_ref[...]
        gate = jnp.dot(xn, wg_ref[0], preferred_element_type=F32)
        up = jnp.dot(xn, wu_ref[0], preferred_element_type=F32)
        act = (gate * _sigmoid(gate) * up).astype(BF16)
        acc_ref[...] += jnp.dot(act, wd_ref[0], preferred_element_type=F32)

    @pl.when(live & (f == last))
    def _():
        o_ref[...] = acc_ref[...]

    @pl.when(jnp.logical_not(live) & (f == last))
    def _():
        o_ref[...] = jnp.zeros_like(o_ref)


def moe_experts(xs, g, wg, wu, wd, tile_expert, tile_live, *, tm, tf=512, name="moe_experts"):
    n_rows, d = xs.shape
    dff = wg.shape[2]
    nf = dff // tf
    fidx = lambda i, f, te, tv: f * tv[i] + (nf - 1) * (1 - tv[i])
    grid_spec = pltpu.PrefetchScalarGridSpec(
        num_scalar_prefetch=2,
        grid=(n_rows // tm, nf),
        in_specs=[pl.BlockSpec((tm, d), lambda i, f, te, tv: (i, 0)),
                  pl.BlockSpec((1, d), lambda i, f, te, tv: (0, 0)),
                  pl.BlockSpec((1, d, tf), lambda i, f, te, tv: (te[i], 0, fidx(i, f, te, tv))),
                  pl.BlockSpec((1, d, tf), lambda i, f, te, tv: (te[i], 0, fidx(i, f, te, tv))),
                  pl.BlockSpec((1, tf, d), lambda i, f, te, tv: (te[i], fidx(i, f, te, tv), 0))],
        out_specs=pl.BlockSpec((tm, d), lambda i, f, te, tv: (i, 0)),
        scratch_shapes=[pltpu.VMEM((tm, d), BF16), pltpu.VMEM((tm, d), F32)])
    return pl.pallas_call(
        _moe_expert_kernel,
        grid_spec=grid_spec,
        out_shape=jax.ShapeDtypeStruct((n_rows, d), F32),
        compiler_params=_cparams("arbitrary", "arbitrary"),
        name=name,
    )(tile_expert, tile_live, xs, g.reshape(1, d), wg, wu, wd)


def _moe_combine_kernel(pos0_ref, pos1_ref, h_ref, route_ref, fn_ref, ys_hbm, o_ref, y0_ref, y1_ref, sem):
    tm = h_ref.shape[0]

    def start(r, c):
        _row_copy(ys_hbm, pos0_ref[0, 0, r], y0_ref, r, sem.at[0]).start()
        _row_copy(ys_hbm, pos1_ref[0, 0, r], y1_ref, r, sem.at[1]).start()
        return c

    def wait(r, c):
        _row_copy(ys_hbm, pos0_ref[0, 0, r], y0_ref, r, sem.at[0]).wait()
        _row_copy(ys_hbm, pos1_ref[0, 0, r], y1_ref, r, sem.at[1]).wait()
        return c

    lax.fori_loop(0, tm, start, 0, unroll=8)
    lax.fori_loop(0, tm, wait, 0, unroll=8)
    route = route_ref[...]
    g1 = route[:, ROUTE_G1:ROUTE_G1 + 1]
    g2 = route[:, ROUTE_G2:ROUTE_G2 + 1]
    o_ref[...] = _rms(h_ref[...] + g1 * y0_ref[...] + g2 * y1_ref[...], fn_ref[...])


def moe_combine(h, route, ys, pos0, pos1, final_g, *, tm=512, name="moe_combine"):
    t, d = h.shape
    idx_spec = pl.BlockSpec((1, 1, tm), lambda i: (i, 0, 0), memory_space=pltpu.SMEM)
    return pl.pallas_call(
        _moe_combine_kernel,
        grid=(t // tm,),
        in_specs=[idx_spec, idx_spec, _row_spec(tm, d), _row_spec(tm, LANES), _full_spec((1, d)),
                  pl.BlockSpec(memory_space=pl.ANY)],
        out_specs=_row_spec(tm, d),
        out_shape=jax.ShapeDtypeStruct((t, d), F32),
        scratch_shapes=[pltpu.VMEM((tm, d), F32), pltpu.VMEM((tm, d), F32), pltpu.SemaphoreType.DMA((2,))],
        compiler_params=_cparams("arbitrary"),
        name=name,
    )(pos0.reshape(t // tm, 1, tm), pos1.reshape(t // tm, 1, tm), h, route, final_g.reshape(1, d), ys)


def moe(h, g, router_pad, wg, wu, wd, final_g, *, tm_rows=1024):
    t, d = h.shape
    n_tiles = 2 * t // tm_rows + N_EXPERTS
    route, counts = moe_route(h, g, router_pad)
    counts = counts[0, :N_EXPERTS].astype(jnp.int32)
    padded = (counts + tm_rows - 1) // tm_rows * tm_rows
    ends = jnp.cumsum(padded)
    offsets = ends - padded
    experts = jnp.arange(N_EXPERTS, dtype=jnp.int32)

    def position(idx_lane, rank_lane):
        e = route[:, idx_lane].astype(jnp.int32)
        off = jnp.sum(jnp.where(e[:, None] == experts[None, :], offsets[None, :], 0), axis=1)
        return off + route[:, rank_lane].astype(jnp.int32)

    pos0, pos1 = position(ROUTE_I1, ROUTE_R1), position(ROUTE_I2, ROUTE_R2)
    tile_start = jnp.arange(n_tiles, dtype=jnp.int32) * tm_rows
    tile_expert = jnp.minimum(jnp.sum(tile_start[:, None] >= ends[None, :], axis=1), N_EXPERTS - 1).astype(jnp.int32)
    tile_live = (tile_start < ends[-1]).astype(jnp.int32)
    xs = moe_dispatch(h, pos0, pos1, n_tiles * tm_rows)
    ys = moe_experts(xs, g, wg, wu, wd, tile_expert, tile_live, tm=tm_rows)
    return moe_combine(h, route, ys, pos0, pos1, final_g)


LOG2E = 1.4426950408889634


def _flash_kernel(qb, kb, q_ref, k_ref, vt_ref, o_ref, acc_ref, sa_ref, sb_ref):
    i = pl.program_id(2)
    key = lax.broadcasted_iota(jnp.int32, (kb, qb), 0)
    qry = lax.broadcasted_iota(jnp.int32, (kb, qb), 1)
    diag_masks = (key <= qry, key + kb <= qry)

    def qk_into(s_ref, blk):
        start = pl.multiple_of(blk * kb, kb)
        for h in range(2):
            k = k_ref[0, pl.ds(start, kb), h * HEAD_LANES:(h + 1) * HEAD_LANES]
            q = q_ref[0, :, h * HEAD_LANES:(h + 1) * HEAD_LANES]
            s_ref[h] = lax.dot_general(k, q, _NT, preferred_element_type=F32)

    def process(s_ref, blk, ms, mask=None):
        new = []
        for h in range(2):
            s = s_ref[h] if mask is None else jnp.where(mask, s_ref[h], NEG_BIG)
            m_new = jnp.maximum(ms[h], jnp.max(s, axis=0, keepdims=True))
            alpha = jnp.exp2(ms[h] - m_new)
            p = jnp.exp2((s - m_new).astype(BF16))
            new.append(m_new)
            acc_ref[h] = alpha * acc_ref[h] + jnp.dot(vt_ref[0, h, blk], p, preferred_element_type=F32)
        return tuple(new)

    acc_ref[...] = jnp.zeros_like(acc_ref)
    qk_into(sa_ref, 0)

    def body(j, ms):
        qk_into(sb_ref, 2 * j + 1)
        ms = process(sa_ref, 2 * j, ms)
        qk_into(sa_ref, 2 * j + 2)
        return process(sb_ref, 2 * j + 1, ms)

    m_init = jnp.full((1, qb), NEG_BIG, F32)
    ms = lax.fori_loop(0, i, body, (m_init, m_init))
    qk_into(sb_ref, 2 * i + 1)
    ms = process(sa_ref, 2 * i, ms, diag_masks[0])
    process(sb_ref, 2 * i + 1, ms, diag_masks[1])
    outs = []
    for h in range(2):
        acc = acc_ref[h]
        outs.append(acc[:HEAD_DIM] / acc[HEAD_DIM:HEAD_DIM + 1])
    o_ref[0] = jnp.concatenate(outs, axis=0).T.astype(o_ref.dtype)


FLASH_V_ROWS = 80


def flash_attention(q, k, v, *, qb=512, name="flash"):
    b, s, hw = q.shape
    n_heads = hw // HEAD_LANES
    kb = qb // 2
    nkb = s // kb
    vt = v.reshape(b, nkb, kb, n_heads, HEAD_DIM).transpose(0, 3, 1, 4, 2)
    pad_rows = FLASH_V_ROWS - HEAD_DIM - 1
    vt = jnp.concatenate([vt, jnp.ones((b, n_heads, nkb, 1, kb), BF16),
                          jnp.zeros((b, n_heads, nkb, pad_rows, kb), BF16)], axis=3)
    return pl.pallas_call(
        functools.partial(_flash_kernel, qb, kb),
        grid=(b, n_heads // 2, s // qb),
        in_specs=[pl.BlockSpec((1, qb, 2 * HEAD_LANES), lambda bi, p, i: (bi, i, p)),
                  pl.BlockSpec((1, s, 2 * HEAD_LANES), lambda bi, p, i: (bi, 0, p)),
                  pl.BlockSpec((1, 2, nkb, FLASH_V_ROWS, kb), lambda bi, p, i: (bi, p, 0, 0, 0))],
        out_specs=pl.BlockSpec((1, qb, 2 * HEAD_DIM), lambda bi, p, i: (bi, i, p)),
        out_shape=jax.ShapeDtypeStruct((b, s, n_heads * HEAD_DIM), BF16),
        scratch_shapes=[pltpu.VMEM((2, FLASH_V_ROWS, qb), F32),
                        pltpu.VMEM((2, kb, qb), F32), pltpu.VMEM((2, kb, qb), F32)],
        compiler_params=_cparams("parallel", "parallel", "arbitrary"),
        name=name,
    )(q, k, vt)


def _rwkv_kernel(prec, x_ref, mu_ref, w0_ref, w2_ref, a0_ref, a2_ref, g2_ref, kk_ref, ka_ref, rk_ref,
                 lng_ref, lnb_ref, o_ref, carry_ref, s_ref):
    c = RW_CHUNK
    gl = RW_GROUP_LANES
    n_groups = GROUP_WIDTH // gl

    @pl.when(pl.program_id(1) == 0)
    def _():
        carry_ref[...] = jnp.zeros_like(carry_ref)
        s_ref[...] = jnp.zeros_like(s_ref)

    x = x_ref[0]
    row = lax.broadcasted_iota(jnp.int32, x.shape, 0)
    prev = jnp.where(row == 0, carry_ref[0:1, :], pltpu.roll(x, 1, axis=0))
    carry_ref[0:1, :] = x[c - 1:c, :]
    xs = x + mu_ref[...] * (prev - x)

    gw = GROUP_WIDTH
    r, k, v = xs[:, 0:gw], xs[:, gw:2 * gw], xs[:, 2 * gw:3 * gw]
    wa = xs[:, 3 * gw:3 * gw + DECAY_LORA + AAA_LORA]
    gd = xs[:, 3 * gw + DECAY_LORA + AAA_LORA:]
    log_w = -_softplus(-(w0_ref[...] + _dot(jnp.tanh(wa), w2_ref[...], 3))) - 0.5
    lw = -jnp.exp(log_w)
    a = _sigmoid(a0_ref[...] + _dot(wa, a2_ref[...], 3))
    g = _dot(_sigmoid(gd), g2_ref[...], 3)

    brow = lax.broadcasted_iota(jnp.int32, (gl, gl), 0) // HEAD_DIM
    bcol = lax.broadcasted_iota(jnp.int32, (gl, gl), 1) // HEAD_DIM
    bmask = brow == bcol
    bones = jnp.where(bmask, 1.0, 0.0).astype(BF16)

    def head_sum(t):
        return jnp.concatenate([_dot(t[:, i * gl:(i + 1) * gl], bones, 2) for i in range(n_groups)], axis=1)

    kk = k * kk_ref[...]
    kk = kk / jnp.maximum(jnp.sqrt(head_sum(kk * kk)), 1e-12)
    k2 = k * (1.0 + (a - 1.0) * ka_ref[...])
    b = kk * a

    ti = lax.broadcasted_iota(jnp.int32, (c, c), 0)
    si = lax.broadcasted_iota(jnp.int32, (c, c), 1)
    log_p = _dot(jnp.where(si <= ti, 1.0, 0.0), lw, 6)
    log_pc = log_p[c - 1:c, :]
    inv_p = jnp.exp(-log_p)
    to_end = jnp.exp(log_pc - log_p)
    a_t = -kk * jnp.exp(log_p - lw)
    r_t = r * jnp.exp(log_p)
    b_t, k_t = b * inv_p, k2 * inv_p
    b_e, k_e = b * to_end, k2 * to_end
    p_c = jnp.exp(log_pc)

    tl = lax.broadcasted_iota(jnp.int32, (c, gl), 0)
    sl = lax.broadcasted_iota(jnp.int32, (c, gl), 1) % HEAD_DIM
    strict, incl = sl < tl, sl <= tl
    eye_l = jnp.where(sl == tl, 1.0, 0.0)
    er = lax.broadcasted_iota(jnp.int32, (gl, gl), 0)
    ec = lax.broadcasted_iota(jnp.int32, (gl, gl), 1)

    def bd(t):
        return jnp.where(bmask, jnp.concatenate([t] * RW_GROUP_HEADS, axis=0), 0.0)

    ys = []
    for gi in range(n_groups):
        sl_g = slice(gi * gl, (gi + 1) * gl)
        at_g, rt_g, bt_g, kt_g = a_t[:, sl_g], r_t[:, sl_g], b_t[:, sl_g], k_t[:, sl_g]
        be_g, ke_g, v_g = b_e[:, sl_g], k_e[:, sl_g], v[:, sl_g]
        gram = _dot_general(jnp.concatenate([at_g, rt_g], axis=0),
                            jnp.concatenate([bd(bt_g), bd(kt_g)], axis=0), _NT, prec)
        a_ab = jnp.where(strict, gram[:c, :gl], 0.0)
        a_ak = jnp.where(strict, gram[:c, gl:], 0.0)
        a_rb = jnp.where(incl, gram[c:, :gl], 0.0)
        a_rk = jnp.where(incl, gram[c:, gl:], 0.0)
        pw = _dot(a_ab, bd(a_ab), prec)
        tinv = eye_l + a_ab
        n_sq = int(np.log2(c)) - 1
        for it in range(n_sq):
            if it + 1 < n_sq:
                both = _dot(jnp.concatenate([tinv, pw], axis=0), bd(pw), prec)
                tinv, pw = tinv + both[:c], both[c:]
            else:
                tinv = tinv + _dot(tinv, bd(pw), prec)
        akv = _dot(jnp.concatenate([a_ak, a_rk], axis=0), bd(v_g), prec)
        wu = _dot(tinv, jnp.concatenate([bd(at_g), bd(akv[:c])], axis=1), prec)
        w_g, u0 = wu[:, :gl], wu[:, gl:]
        rwu = _dot(a_rb, jnp.concatenate([bd(w_g), bd(u0)], axis=1), prec)
        r_hat = rt_g + rwu[:, :gl]
        y0 = akv[c:] + rwu[:, gl:]
        m_bd = (jnp.where(bmask, _dot_general(be_g, w_g, _TN, prec), 0.0)
                + jnp.where(er == ec, p_c[:, sl_g], 0.0))
        z_bd = jnp.where(bmask, _dot_general(jnp.concatenate([ke_g, be_g], axis=0),
                                             jnp.concatenate([v_g, u0], axis=0), _TN, prec), 0.0)
        s0 = s_ref[gi]
        ys.append(_dot(r_hat, s0, 3) + y0)
        s_ref[gi] = _dot(m_bd, s0, 3) + z_bd
    y = jnp.concatenate(ys, axis=1)

    inv_n = 1.0 / HEAD_DIM
    d = y - head_sum(y) * inv_n
    yn = d * lax.rsqrt(head_sum(d * d) * inv_n + RW_GN_EPS) * lng_ref[...] + lnb_ref[...]
    bonus = head_sum(r * k2 * rk_ref[...]) * v
    o_ref[0] = ((yn + bonus) * g).astype(o_ref.dtype)


def rwkv7(rw, mu, w0, w2, a0, a2, g2, k_k, k_a, r_k, lnx_g, lnx_b, *, prec=1, name="rwkv7"):
    b, s, cols = rw.shape
    gw = GROUP_WIDTH
    zeros = jnp.zeros((DECAY_LORA, gw), F32)
    w2p = jnp.concatenate([w2, zeros], axis=0)
    a2p = jnp.concatenate([zeros, a2], axis=0)
    vec = lambda t: t.reshape(1, -1).astype(F32)
    params = [vec(mu), vec(w0), w2p, vec(a0), a2p, g2, vec(k_k), vec(k_a), vec(r_k), vec(lnx_g), vec(lnx_b)]
    return pl.pallas_call(
        functools.partial(_rwkv_kernel, prec),
        grid=(b, s // RW_CHUNK),
        in_specs=[pl.BlockSpec((1, RW_CHUNK, cols), lambda bi, ci: (bi, ci, 0))]
                 + [pl.BlockSpec(p.shape, lambda bi, ci: (0, 0)) for p in params],
        out_specs=pl.BlockSpec((1, RW_CHUNK, gw), lambda bi, ci: (bi, ci, 0)),
        out_shape=jax.ShapeDtypeStruct((b, s, gw), BF16),
        scratch_shapes=[pltpu.VMEM((8, cols), F32),
                        pltpu.VMEM((gw // RW_GROUP_LANES, RW_GROUP_LANES, RW_GROUP_LANES), F32)],
        compiler_params=_cparams("parallel", "arbitrary"),
        name=name,
    )(rw, *params)


def _tile_lanes(t, n):
    return jnp.concatenate([t] * n, axis=1)


def _expand_matrix():
    e = np.zeros((GROUP_WIDTH, N_HEADS * HEAD_LANES), np.float32)
    for h in range(N_HEADS):
        e[h * HEAD_DIM + np.arange(HEAD_DIM), h * HEAD_LANES + np.arange(HEAD_DIM)] = 1.0
    return jnp.asarray(e, BF16)


def _mla_prep_kernel(x_ref, qn_ref, kvn_ref, wq_ref, wqr_ref, wk_ref, wv_ref,
                     cq_ref, sq_ref, ck_ref, sk_ref, q_ref, k_ref, v_ref):
    x = x_ref[...]
    q_lat = x[:, :MLA_Q_RANK]
    kv_lat = x[:, MLA_Q_RANK:MLA_Q_RANK + MLA_KV_RANK]
    kpe = x[:, MLA_Q_RANK + MLA_KV_RANK:MLA_Q_RANK + MLA_KV_RANK + HEAD_LANES]
    kpe_rot = x[:, MLA_Q_RANK + MLA_KV_RANK + HEAD_LANES:]
    qn = _rms(q_lat, qn_ref[...]).astype(BF16)
    q = (jnp.dot(qn, wq_ref[...], preferred_element_type=F32) * _tile_lanes(cq_ref[...], N_HEADS)
         + jnp.dot(qn, wqr_ref[...], preferred_element_type=F32) * _tile_lanes(sq_ref[...], N_HEADS))
    q_ref[...] = q.astype(BF16)
    kn = _rms(kv_lat, kvn_ref[...]).astype(BF16)
    k_rope = kpe * ck_ref[...] + kpe_rot * sk_ref[...]
    k = jnp.dot(kn, wk_ref[...], preferred_element_type=F32) + _tile_lanes(k_rope, N_HEADS)
    k_ref[...] = k.astype(BF16)
    v_ref[...] = jnp.dot(kn, wv_ref[...], preferred_element_type=F32).astype(BF16)


def mla_prep(x, q_norm, kv_norm, w_uq, w_ukv, seq_len, *, tm=256, name="mla_prep"):
    t = x.shape[0]
    nope, rope, half = HEAD_DIM, MLA_ROPE, MLA_ROPE // 2
    wq = w_uq.reshape(MLA_Q_RANK, N_HEADS, nope + rope)
    wq_pe = wq[:, :, nope:]
    pad = lambda a: jnp.pad(a, ((0, 0), (0, 0), (0, HEAD_LANES - a.shape[2])))
    flat = lambda a: a.reshape(a.shape[0], -1).astype(BF16)
    wq_aug = flat(pad(wq))
    wq_rot = flat(pad(jnp.concatenate([jnp.zeros_like(wq[:, :, :nope]), -wq_pe[:, :, half:], wq_pe[:, :, :half]], axis=2)))
    wkv = w_ukv.reshape(MLA_KV_RANK, N_HEADS, 2 * HEAD_DIM)
    wk_aug = flat(pad(wkv[:, :, :nope]))
    wv = flat(wkv[:, :, nope:])

    scale = float(nope + rope) ** -0.5 * LOG2E
    inv_freq = ROPE_THETA ** (-jnp.arange(0, rope, 2, dtype=F32) / rope)
    ang = jnp.arange(seq_len, dtype=F32)[:, None] * inv_freq[None, :]
    cos2, sin2 = _tile_lanes(jnp.cos(ang), 2), _tile_lanes(jnp.sin(ang), 2)
    z64, z32 = jnp.zeros((seq_len, nope), F32), jnp.zeros((seq_len, HEAD_LANES - nope - rope), F32)
    cq = jnp.concatenate([jnp.full((seq_len, nope), scale, F32), cos2 * scale, z32], axis=1)
    sq = jnp.concatenate([z64, sin2 * scale, z32], axis=1)
    ck = jnp.concatenate([z64, cos2, z32], axis=1)
    sk = jnp.concatenate([z64, sin2, z32], axis=1)

    n_seq_tiles = seq_len // tm
    tab_spec = pl.BlockSpec((tm, HEAD_LANES), lambda i: (i % n_seq_tiles, 0))
    weights = [wq_aug, wq_rot, wk_aug, wv]
    return pl.pallas_call(
        _mla_prep_kernel,
        grid=(t // tm,),
        in_specs=[_row_spec(tm, x.shape[1]), _full_spec((1, MLA_Q_RANK)), _full_spec((1, MLA_KV_RANK))]
                 + [_full_spec(w.shape) for w in weights] + [tab_spec] * 4,
        out_specs=[_row_spec(tm, N_HEADS * HEAD_LANES), _row_spec(tm, N_HEADS * HEAD_LANES),
                   _row_spec(tm, GROUP_WIDTH)],
        out_shape=[jax.ShapeDtypeStruct((t, N_HEADS * HEAD_LANES), BF16),
                   jax.ShapeDtypeStruct((t, N_HEADS * HEAD_LANES), BF16),
                   jax.ShapeDtypeStruct((t, GROUP_WIDTH), BF16)],
        compiler_params=_cparams("parallel"),
        name=name,
    )(x, q_norm.reshape(1, -1), kv_norm.reshape(1, -1), *weights, cq, sq, ck, sk)


FOX_PARTS = 3


def _fox_prep_kernel(scale, q_ref, k_ref, f_ref, bf_ref, e_ref, pq_ref, pk_ref, cq_ref, ck_ref,
                     qo_ref, ko_ref, carry_ref):
    @pl.when(pl.program_id(1) == 0)
    def _():
        carry_ref[...] = jnp.zeros_like(carry_ref)

    tm = q_ref.shape[1]
    log_f = -_softplus(-(f_ref[0] + bf_ref[...]))
    ti = lax.broadcasted_iota(jnp.int32, (tm, tm), 0)
    si = lax.broadcasted_iota(jnp.int32, (tm, tm), 1)
    d = _dot(jnp.where(si <= ti, 1.0, 0.0), log_f, 6) + carry_ref[0:1, :]
    carry_ref[0:1, :] = d[tm - 1:tm, :]
    d2 = d * LOG2E
    hi = d2.astype(BF16)
    rem = d2 - hi.astype(F32)
    mid = rem.astype(BF16)
    lo = (rem - mid.astype(F32)).astype(BF16)
    parts = jnp.concatenate([hi, mid, lo], axis=1)
    q = (jnp.dot((q_ref[0] * scale).astype(BF16), e_ref[...], preferred_element_type=F32)
         + jnp.dot(parts, pq_ref[...], preferred_element_type=F32) + cq_ref[...])
    k = (jnp.dot(k_ref[0].astype(BF16), e_ref[...], preferred_element_type=F32)
         + jnp.dot(parts, pk_ref[...], preferred_element_type=F32) + ck_ref[...])
    qo_ref[0] = q.astype(BF16)
    ko_ref[0] = k.astype(BF16)


def fox_prep(q, k, f_logit, b_f, *, tm=256, name="fox_prep"):
    b, s, _ = q.shape
    width = N_HEADS * HEAD_LANES
    pq = np.zeros((FOX_PARTS * LANES, width), np.float32)
    pk = np.zeros((FOX_PARTS * LANES, width), np.float32)
    cq = np.zeros((1, width), np.float32)
    ck = np.zeros((1, width), np.float32)
    for h in range(N_HEADS):
        base = h * HEAD_LANES + HEAD_DIM
        for p in range(FOX_PARTS):
            pk[p * LANES + h, base + p] = -1.0
            pq[p * LANES + h, base + FOX_PARTS + p] = 1.0
            cq[0, base + p] = 1.0
            ck[0, base + FOX_PARTS + p] = 1.0
    consts = [_expand_matrix(), jnp.asarray(pq, BF16), jnp.asarray(pk, BF16), jnp.asarray(cq), jnp.asarray(ck)]
    bf_pad = jnp.zeros((1, LANES), F32).at[0, :N_HEADS].set(b_f.astype(F32))
    blk = lambda n: pl.BlockSpec((1, tm, n), lambda bi, i: (bi, i, 0))
    return pl.pallas_call(
        functools.partial(_fox_prep_kernel, float(HEAD_DIM) ** -0.5 * LOG2E),
        grid=(b, s // tm),
        in_specs=[blk(GROUP_WIDTH), blk(GROUP_WIDTH), blk(LANES), pl.BlockSpec((1, LANES), lambda bi, i: (0, 0))]
                 + [pl.BlockSpec(c.shape, lambda bi, i: (0, 0)) for c in consts],
        out_specs=[blk(width), blk(width)],
        out_shape=[jax.ShapeDtypeStruct((b, s, width), BF16)] * 2,
        scratch_shapes=[pltpu.VMEM((8, LANES), F32)],
        compiler_params=_cparams("parallel", "arbitrary"),
        name=name,
    )(q, k, f_logit, bf_pad, *consts)


MOBA_NB_PAD = 32


def _rope_full(x, cos, sin):
    half = HEAD_DIM // 2
    lane = lax.broadcasted_iota(jnp.int32, x.shape, 1)
    width = x.shape[1]
    rot = jnp.where(lane % HEAD_DIM < half, -pltpu.roll(x, width - half, axis=1), pltpu.roll(x, half, axis=1))
    return x * cos + rot * sin


def _moba_prep_kernel(q_ref, k_ref, cos_ref, sin_ref, e_ref, qo_ref, ko_ref, km_ref):
    blk = pl.program_id(1)
    cos = _tile_lanes(cos_ref[...], GROUP_WIDTH // LANES)
    sin = _tile_lanes(sin_ref[...], GROUP_WIDTH // LANES)
    qo_ref[0] = _rope_full(q_ref[0], cos, sin)
    k = _rope_full(k_ref[0], cos, sin)
    km_ref[0, 0] = jnp.mean(k, axis=0, keepdims=True)
    lane = lax.broadcasted_iota(jnp.int32, ko_ref.shape[1:], 1) % HEAD_LANES
    own_lane = jnp.where(lane == HEAD_DIM + blk, 1.0, 0.0)
    ko_ref[0] = (jnp.dot(k.astype(BF16), e_ref[...], preferred_element_type=F32) + own_lane).astype(BF16)


def moba_prep(q, k, *, name="moba_prep"):
    b, s, _ = q.shape
    tm = MOBA_BLOCK
    nb = s // tm
    assert nb <= MOBA_NB_PAD
    inv_freq = ROPE_THETA ** (-jnp.arange(0, HEAD_DIM, 2, dtype=F32) / HEAD_DIM)
    ang = jnp.arange(s, dtype=F32)[:, None] * inv_freq[None, :]
    cos, sin = _tile_lanes(jnp.cos(ang), LANES // (HEAD_DIM // 2)), _tile_lanes(jnp.sin(ang), LANES // (HEAD_DIM // 2))
    width = N_HEADS * HEAD_LANES
    blk = lambda n: pl.BlockSpec((1, tm, n), lambda bi, i: (bi, i, 0))
    tab = pl.BlockSpec((tm, LANES), lambda bi, i: (i, 0))
    e = _expand_matrix()
    return pl.pallas_call(
        _moba_prep_kernel,
        grid=(b, nb),
        in_specs=[blk(GROUP_WIDTH), blk(GROUP_WIDTH), tab, tab, pl.BlockSpec(e.shape, lambda bi, i: (0, 0))],
        out_specs=[blk(GROUP_WIDTH), blk(width),
                   pl.BlockSpec((1, 1, 1, GROUP_WIDTH), lambda bi, i: (bi, i, 0, 0))],
        out_shape=[jax.ShapeDtypeStruct((b, s, GROUP_WIDTH), F32), jax.ShapeDtypeStruct((b, s, width), BF16),
                   jax.ShapeDtypeStruct((b, nb, 1, GROUP_WIDTH), F32)],
        compiler_params=_cparams("parallel", "parallel"),
        name=name,
    )(q, k, cos, sin, e)


def _moba_gate_kernel(scale, q_ref, km_ref, e_ref, pm_ref, qo_ref):
    own = pl.program_id(1)
    q = q_ref[0]
    gate_t = _dot_general(km_ref[0], q, _NT, 6)
    nbp = MOBA_NB_PAD
    j = lax.broadcasted_iota(jnp.int32, (nbp, q.shape[0]), 0)
    valid = j < own
    lowest = -3.0e38
    bias_t = []
    for h in range(N_HEADS):
        g = jnp.where(valid, gate_t[h * nbp:(h + 1) * nbp, :], lowest)
        rest = g
        for _ in range(MOBA_TOPK - 1):
            rest = jnp.where(rest >= jnp.max(rest, axis=0, keepdims=True), lowest, rest)
        kth = jnp.max(rest, axis=0, keepdims=True)
        keep = (valid & (g >= kth)) | (j == own)
        bias_t.append(jnp.where(keep, 0.0, NEG_BIG))
    bias = jnp.concatenate(bias_t, axis=0).T
    out = (jnp.dot((q * scale).astype(BF16), e_ref[...], preferred_element_type=F32)
           + jnp.dot(bias.astype(BF16), pm_ref[...], preferred_element_type=F32))
    qo_ref[0] = out.astype(BF16)


def moba_gate(q_rope, k_mean, *, name="moba_gate"):
    b, s, _ = q_rope.shape
    tm = MOBA_BLOCK
    nb = s // tm
    nbp = MOBA_NB_PAD
    width = N_HEADS * HEAD_LANES
    km = k_mean.reshape(b, nb, N_HEADS, HEAD_DIM).transpose(0, 2, 1, 3)
    km = jnp.pad(km, ((0, 0), (0, 0), (0, nbp - nb), (0, 0)))
    eye = jnp.eye(N_HEADS, dtype=F32)
    km_bd = (km[:, :, :, None, :] * eye[None, :, None, :, None]).reshape(b, N_HEADS * nbp, GROUP_WIDTH)
    pm = np.zeros((N_HEADS * nbp, width), np.float32)
    for h in range(N_HEADS):
        pm[h * nbp + np.arange(nbp), h * HEAD_LANES + HEAD_DIM + np.arange(nbp)] = 1.0
    e, pm = _expand_matrix(), jnp.asarray(pm, BF16)
    return pl.pallas_call(
        functools.partial(_moba_gate_kernel, float(HEAD_DIM) ** -0.5 * LOG2E),
        grid=(b, nb),
        in_specs=[pl.BlockSpec((1, tm, GROUP_WIDTH), lambda bi, i: (bi, i, 0)),
                  pl.BlockSpec((1, N_HEADS * nbp, GROUP_WIDTH), lambda bi, i: (bi, 0, 0)),
                  pl.BlockSpec(e.shape, lambda bi, i: (0, 0)), pl.BlockSpec(pm.shape, lambda bi, i: (0, 0))],
        out_specs=pl.BlockSpec((1, tm, width), lambda bi, i: (bi, i, 0)),
        out_shape=jax.ShapeDtypeStruct((b, s, width), BF16),
        compiler_params=_cparams("parallel", "parallel"),
        name=name,
    )(q_rope, km_bd, e, pm)


def kernel(x, norm_mix_0, w_in_0, shift_mu_0, rw_w0_0, rw_w2_0, rw_a0_0, rw_a2_0, rw_g2_0, rw_kk_0, rw_ka_0,
           rw_rk_0, rw_lnx_g_0, rw_lnx_b_0, mla_qnorm_0, mla_wuq_0, mla_kvnorm_0, mla_wukv_0, w_out_0,
           norm_ffn_0, ffn_wg_0, ffn_wu_0, ffn_wd_0, norm_mix_1, w_in_1, fox_bf_1, w_out_1, norm_ffn_1,
           router_1, moe_wg_1, moe_wu_1, moe_wd_1, final_norm):
    b, s, d = x.shape
    t = b * s
    gw = GROUP_WIDTH
    bf = lambda w: w.astype(BF16)
    x2 = x.reshape(t, d)

    mla0 = RW_COLS
    w_q, w_kv = w_in_0[:, mla0:mla0 + MLA_Q_RANK], w_in_0[:, mla0 + MLA_Q_RANK:mla0 + MLA_Q_RANK + MLA_KV_RANK]
    w_kr = w_in_0[:, mla0 + MLA_Q_RANK + MLA_KV_RANK:]
    half = MLA_ROPE // 2
    z = lambda n: jnp.zeros((d, n), F32)
    w_kpe = jnp.concatenate([z(HEAD_DIM), w_kr, z(HEAD_LANES - HEAD_DIM - MLA_ROPE)], axis=1)
    w_kpe_rot = jnp.concatenate([z(HEAD_DIM), -w_kr[:, half:], w_kr[:, :half],
                                 z(HEAD_LANES - HEAD_DIM - MLA_ROPE)], axis=1)
    w_mla = jnp.concatenate([w_q, w_kv, w_kpe, w_kpe_rot], axis=1)
    rw, mla = norm_proj(x2, norm_mix_0, [bf(w_in_0[:, :RW_COLS]), bf(w_mla)], [F32, F32], name="in_proj_0")
    y_a = rwkv7(rw.reshape(b, s, RW_COLS), shift_mu_0, rw_w0_0, rw_w2_0, rw_a0_0, rw_a2_0, rw_g2_0,
                rw_kk_0, rw_ka_0, rw_rk_0, rw_lnx_g_0, rw_lnx_b_0)
    q, k, v = mla_prep(mla, mla_qnorm_0, mla_kvnorm_0, mla_wuq_0, mla_wukv_0, s)
    y_b = flash_attention(q.reshape(b, s, -1), k.reshape(b, s, -1), v.reshape(b, s, -1), name="flash_mla")
    h = out_proj(y_a.reshape(t, gw), y_b.reshape(t, gw), bf(w_out_0[:gw]), bf(w_out_0[gw:]), x2, name="out_proj_0")
    h = ffn(h, norm_ffn_0, bf(ffn_wg_0), bf(ffn_wu_0), bf(ffn_wd_0))

    c0 = 3 * gw
    w_f = jnp.pad(w_in_1[:, c0:c0 + N_HEADS], ((0, 0), (0, LANES - N_HEADS)))
    c1 = c0 + N_HEADS
    cols = [w_in_1[:, 0:gw], w_in_1[:, gw:2 * gw], w_in_1[:, 2 * gw:3 * gw], w_f,
            w_in_1[:, c1:c1 + gw], w_in_1[:, c1 + gw:c1 + 2 * gw], w_in_1[:, c1 + 2 * gw:]]
    fq, fk, fv, ff, mq, mk, mv = norm_proj(h, norm_mix_1, [bf(w) for w in cols],
                                           [F32, F32, BF16, F32, F32, F32, BF16], name="in_proj_1")
    r3 = lambda a: a.reshape(b, s, -1)
    fqa, fka = fox_prep(r3(fq), r3(fk), r3(ff), fox_bf_1)
    y_c = flash_attention(fqa, fka, r3(fv), name="flash_fox")
    mq_rope, mka, k_mean = moba_prep(r3(mq), r3(mk))
    mqa = moba_gate(mq_rope, k_mean)
    y_d = flash_attention(mqa, mka, r3(mv), name="flash_moba")
    h = out_proj(y_c.reshape(t, gw), y_d.reshape(t, gw), bf(w_out_1[:gw]), bf(w_out_1[gw:]), h, name="out_proj_1")

    router_pad = jnp.pad(router_1, ((0, 0), (0, LANES - N_EXPERTS)))
    out = moe(h, norm_ffn_1, router_pad, bf(moe_wg_1), bf(moe_wu_1), bf(moe_wd_1), final_norm)
    return out.reshape(b, s, d)
```

```python
import functools

import numpy as np
import jax
import jax.numpy as jnp
from jax import lax
from jax.experimental import pallas as pl
from jax.experimental.pallas import tpu as pltpu

F32 = jnp.float32
BF16 = jnp.bfloat16

HEAD_DIM = 64
N_HEADS = 8
GROUP_WIDTH = N_HEADS * HEAD_DIM
HEAD_LANES = 128
MLA_ROPE = 32
MLA_Q_RANK = 256
MLA_KV_RANK = 128
DECAY_LORA = 64
AAA_LORA = 64
GATE_LORA = 128
RW_COLS = 3 * GROUP_WIDTH + DECAY_LORA + AAA_LORA + GATE_LORA
RW_GN_EPS = 64e-5
MOBA_BLOCK = 256
MOBA_TOPK = 3
ROPE_THETA = 10000.0
NORM_EPS = 1e-6
N_EXPERTS = 8
NEG_BIG = -1e30

LANES = 128
VMEM_LIMIT_BYTES = 56 * 1024 * 1024

RW_CHUNK = 64
RW_GROUP_HEADS = 4
RW_GROUP_LANES = RW_GROUP_HEADS * HEAD_DIM
RW_BATCH = 4


def _cparams(*semantics):
    return pltpu.CompilerParams(dimension_semantics=semantics, vmem_limit_bytes=VMEM_LIMIT_BYTES)


def _dot(a, b, prec=1):
    if prec == 6:
        return jnp.dot(a.astype(F32), b.astype(F32), preferred_element_type=F32,
                       precision=lax.Precision.HIGHEST)
    if prec == 1:
        return jnp.dot(a.astype(BF16), b.astype(BF16), preferred_element_type=F32)
    ah, al = _split(a)
    if b.dtype == BF16:
        return jnp.dot(ah, b, preferred_element_type=F32) + jnp.dot(al, b, preferred_element_type=F32)
    bh, bl = _split(b)
    return (jnp.dot(ah, bh, preferred_element_type=F32) + jnp.dot(ah, bl, preferred_element_type=F32)
            + jnp.dot(al, bh, preferred_element_type=F32))


def _split(x):
    hi = x.astype(BF16)
    return hi, (x - hi.astype(F32)).astype(BF16)


_NT = (((1,), (1,)), ((), ()))
_TN = (((0,), (0,)), ((), ()))


def _dot_general(a, b, dims, prec=1):
    if prec == 6:
        return lax.dot_general(a.astype(F32), b.astype(F32), dims, preferred_element_type=F32,
                               precision=lax.Precision.HIGHEST)
    if prec == 1:
        return lax.dot_general(a.astype(BF16), b.astype(BF16), dims, preferred_element_type=F32)
    ah, al = _split(a)
    bh, bl = _split(b)
    return (lax.dot_general(ah, bh, dims, preferred_element_type=F32)
            + lax.dot_general(ah, bl, dims, preferred_element_type=F32)
            + lax.dot_general(al, bh, dims, preferred_element_type=F32))


def _rms(x, g):
    return x * lax.rsqrt(jnp.mean(x * x, axis=-1, keepdims=True) + NORM_EPS) * g


def _softplus(x):
    return jnp.maximum(x, 0.0) + jnp.log(1.0 + jnp.exp(-jnp.abs(x)))


def _sigmoid(x):
    return 1.0 / (1.0 + jnp.exp(-x))


def _row_spec(tm, n):
    return pl.BlockSpec((tm, n), lambda i: (i, 0))


def _full_spec(shape):
    nd = len(shape)
    return pl.BlockSpec(shape, lambda *_: (0,) * nd)


def _norm_proj_kernel(n_out, x_ref, g_ref, *refs):
    w_refs, o_refs = refs[:n_out], refs[n_out:]
    xn = _rms(x_ref[...], g_ref[...]).astype(BF16)
    for w_ref, o_ref in zip(w_refs, o_refs):
        o_ref[...] = jnp.dot(xn, w_ref[...], preferred_element_type=F32).astype(o_ref.dtype)


def norm_proj(x, g, weights, dtypes, *, tm=256, name="norm_proj"):
    t, d = x.shape
    n_out = len(weights)
    return pl.pallas_call(
        functools.partial(_norm_proj_kernel, n_out),
        grid=(t // tm,),
        in_specs=[_row_spec(tm, d), _full_spec((1, d))] + [_full_spec(w.shape) for w in weights],
        out_specs=[_row_spec(tm, w.shape[1]) for w in weights],
        out_shape=[jax.ShapeDtypeStruct((t, w.shape[1]), dt) for w, dt in zip(weights, dtypes)],
        compiler_params=_cparams("parallel"),
        name=name,
    )(x, g.reshape(1, d), *weights)


def _out_proj_kernel(y1_ref, y2_ref, w1_ref, w2_ref, r_ref, o_ref):
    o_ref[...] = (r_ref[...] + jnp.dot(y1_ref[...], w1_ref[...], preferred_element_type=F32)
                  + jnp.dot(y2_ref[...], w2_ref[...], preferred_element_type=F32))


def out_proj(y1, y2, w1, w2, resid, *, tm=512, name="out_proj"):
    t, d = resid.shape
    return pl.pallas_call(
        _out_proj_kernel,
        grid=(t // tm,),
        in_specs=[_row_spec(tm, y1.shape[1]), _row_spec(tm, y2.shape[1]),
                  _full_spec(w1.shape), _full_spec(w2.shape), _row_spec(tm, d)],
        out_specs=_row_spec(tm, d),
        out_shape=jax.ShapeDtypeStruct((t, d), F32),
        compiler_params=_cparams("parallel"),
        name=name,
    )(y1, y2, w1, w2, resid)


def _ffn_kernel(h_ref, g_ref, wg_ref, wu_ref, wd_ref, o_ref, xn_ref, acc_ref):
    f = pl.program_id(1)

    @pl.when(f == 0)
    def _():
        xn_ref[...] = _rms(h_ref[...], g_ref[...]).astype(BF16)
        acc_ref[...] = jnp.zeros_like(acc_ref)

    xn = xn_ref[...]
    gate = jnp.dot(xn, wg_ref[...], preferred_element_type=F32)
    up = jnp.dot(xn, wu_ref[...], preferred_element_type=F32)
    act = (gate * _sigmoid(gate) * up).astype(BF16)
    acc_ref[...] += jnp.dot(act, wd_ref[...], preferred_element_type=F32)

    @pl.when(f == pl.num_programs(1) - 1)
    def _():
        o_ref[...] = h_ref[...] + acc_ref[...]


def ffn(h, g, wg, wu, wd, *, tm=1024, tf=256, name="ffn"):
    t, d = h.shape
    dff = wg.shape[1]
    return pl.pallas_call(
        _ffn_kernel,
        grid=(t // tm, dff // tf),
        in_specs=[pl.BlockSpec((tm, d), lambda i, f: (i, 0)),
                  pl.BlockSpec((1, d), lambda i, f: (0, 0)),
                  pl.BlockSpec((d, tf), lambda i, f: (0, f)),
                  pl.BlockSpec((d, tf), lambda i, f: (0, f)),
                  pl.BlockSpec((tf, d), lambda i, f: (f, 0))],
        out_specs=pl.BlockSpec((tm, d), lambda i, f: (i, 0)),
        out_shape=jax.ShapeDtypeStruct((t, d), F32),
        scratch_shapes=[pltpu.VMEM((tm, d), BF16), pltpu.VMEM((tm, d), F32)],
        compiler_params=_cparams("parallel", "arbitrary"),
        name=name,
    )(h, g.reshape(1, d), wg, wu, wd)


ROUTE_I1, ROUTE_I2, ROUTE_G1, ROUTE_G2, ROUTE_R1, ROUTE_R2 = range(6)


def _moe_route_kernel(h_ref, g_ref, router_ref, route_ref, counts_ref, carry_ref):
    @pl.when(pl.program_id(0) == 0)
    def _():
        carry_ref[...] = jnp.zeros_like(carry_ref)

    tm = h_ref.shape[0]
    lane = lax.broadcasted_iota(jnp.int32, (tm, LANES), 1)
    t = _rms(h_ref[...], g_ref[...])
    logits = _dot(t, router_ref[...], prec=6)
    logits = jnp.where(lane < N_EXPERTS, logits, NEG_BIG)
    m1 = jnp.max(logits, axis=-1, keepdims=True)
    i1 = jnp.min(jnp.where(logits == m1, lane, LANES), axis=-1, keepdims=True)
    rest = jnp.where(lane == i1, NEG_BIG, logits)
    m2 = jnp.max(rest, axis=-1, keepdims=True)
    i2 = jnp.min(jnp.where(rest == m2, lane, LANES), axis=-1, keepdims=True)
    ex = jnp.exp(m2 - m1)
    g1 = 1.0 / (1.0 + ex)
    sel = jnp.where((lane == i1) | (lane == i2), 1.0, 0.0)
    ti = lax.broadcasted_iota(jnp.int32, (tm, tm), 0)
    si = lax.broadcasted_iota(jnp.int32, (tm, tm), 1)
    before = jnp.where(si < ti, 1.0, 0.0).astype(BF16)
    rank = jnp.dot(before, sel.astype(BF16), preferred_element_type=F32) + carry_ref[0:1, :]
    carry_ref[0:1, :] = carry_ref[0:1, :] + jnp.sum(sel, axis=0, keepdims=True)
    r1 = jnp.sum(jnp.where(lane == i1, rank, 0.0), axis=-1, keepdims=True)
    r2 = jnp.sum(jnp.where(lane == i2, rank, 0.0), axis=-1, keepdims=True)
    fields = [i1.astype(F32), i2.astype(F32), g1, ex * g1, r1, r2]
    route = jnp.zeros((tm, LANES), F32)
    for idx, val in enumerate(fields):
        route = jnp.where(lane == idx, val, route)
    route_ref[...] = route
    counts_ref[...] = carry_ref[...]


def moe_route(h, g, router_pad, *, tm=512, name="moe_route"):
    t, d = h.shape
    return pl.pallas_call(
        _moe_route_kernel,
        grid=(t // tm,),
        in_specs=[_row_spec(tm, d), _full_spec((1, d)), _full_spec((d, LANES))],
        out_specs=[_row_spec(tm, LANES), _full_spec((8, LANES))],
        out_shape=[jax.ShapeDtypeStruct((t, LANES), F32), jax.ShapeDtypeStruct((8, LANES), F32)],
        scratch_shapes=[pltpu.VMEM((8, LANES), F32)],
        compiler_params=_cparams("arbitrary"),
        name=name,
    )(h, g.reshape(1, d), router_pad)


def _row_copy(src_hbm, src_row, dst, dst_row, sem):
    return pltpu.make_async_copy(src_hbm.at[pl.ds(src_row, 1)], dst.at[pl.ds(dst_row, 1)], sem)


def _moe_dispatch_kernel(pos0_ref, pos1_ref, h_ref, init_hbm, xs_hbm, sem):
    del init_hbm
    tm = h_ref.shape[0]

    def start(r, c):
        _row_copy(h_ref, r, xs_hbm, pos0_ref[0, 0, r], sem.at[0]).start()
        _row_copy(h_ref, r, xs_hbm, pos1_ref[0, 0, r], sem.at[1]).start()
        return c

    def wait(r, c):
        _row_copy(h_ref, r, xs_hbm, pos0_ref[0, 0, r], sem.at[0]).wait()
        _row_copy(h_ref, r, xs_hbm, pos1_ref[0, 0, r], sem.at[1]).wait()
        return c

    lax.fori_loop(0, tm, start, 0, unroll=8)
    lax.fori_loop(0, tm, wait, 0, unroll=8)


def moe_dispatch(h, pos0, pos1, n_rows, *, tm=1024, name="moe_dispatch"):
    t, d = h.shape
    idx_spec = pl.BlockSpec((1, 1, tm), lambda i: (i, 0, 0), memory_space=pltpu.SMEM)
    any_spec = pl.BlockSpec(memory_space=pl.ANY)
    return pl.pallas_call(
        _moe_dispatch_kernel,
        grid=(t // tm,),
        in_specs=[idx_spec, idx_spec, _row_spec(tm, d), any_spec],
        out_specs=any_spec,
        out_shape=jax.ShapeDtypeStruct((n_rows, d), F32),
        scratch_shapes=[pltpu.SemaphoreType.DMA((2,))],
        input_output_aliases={3: 0},
        compiler_params=_cparams("arbitrary"),
        name=name,
    )(pos0.reshape(t // tm, 1, tm), pos1.reshape(t // tm, 1, tm), h, jnp.zeros((n_rows, d), F32))


def _moe_expert_kernel(te_ref, tv_ref, x_ref, g_ref, wg_ref, wu_ref, wd_ref, o_ref, xn_ref, acc_ref):
    i = pl.program_id(0)
    f = pl.program_id(1)
    last = pl.num_programs(1) - 1
    live = tv_ref[i] == 1

    @pl.when(live & (f == 0))
    def _():
        xn_ref[...] = _rms(x_ref[...], g_ref[...]).astype(BF16)
        acc_ref[...] = jnp.zeros_like(acc_ref)

    @pl.when(live)
    def _():
        xn = xn_ref[...]
        gate = jnp.dot(xn, wg_ref[0], preferred_element_type=F32)
        up = jnp.dot(xn, wu_ref[0], preferred_element_type=F32)
        act = (gate * _sigmoid(gate) * up).astype(BF16)
        acc_ref[...] += jnp.dot(act, wd_ref[0], preferred_element_type=F32)

    @pl.when(live & (f == last))
    def _():
        o_ref[...] = acc_ref[...]

    @pl.when(jnp.logical_not(live) & (f == last))
    def _():
        o_ref[...] = jnp.zeros_like(o_ref)


def moe_experts(xs, g, wg, wu, wd, tile_expert, tile_live, *, tm, tf=512, name="moe_experts"):
    n_rows, d = xs.shape
    dff = wg.shape[2]
    nf = dff // tf
    fidx = lambda i, f, te, tv: f * tv[i] + (nf - 1) * (1 - tv[i])
    grid_spec = pltpu.PrefetchScalarGridSpec(
        num_scalar_prefetch=2,
        grid=(n_rows // tm, nf),
        in_specs=[pl.BlockSpec((tm, d), lambda i, f, te, tv: (i, 0)),
                  pl.BlockSpec((1, d), lambda i, f, te, tv: (0, 0)),
                  pl.BlockSpec((1, d, tf), lambda i, f, te, tv: (te[i], 0, fidx(i, f, te, tv))),
                  pl.BlockSpec((1, d, tf), lambda i, f, te, tv: (te[i], 0, fidx(i, f, te, tv))),
                  pl.BlockSpec((1, tf, d), lambda i, f, te, tv: (te[i], fidx(i, f, te, tv), 0))],
        out_specs=pl.BlockSpec((tm, d), lambda i, f, te, tv: (i, 0)),
        scratch_shapes=[pltpu.VMEM((tm, d), BF16), pltpu.VMEM((tm, d), F32)])
    return pl.pallas_call(
        _moe_expert_kernel,
        grid_spec=grid_spec,
        out_shape=jax.ShapeDtypeStruct((n_rows, d), F32),
        compiler_params=_cparams("arbitrary", "arbitrary"),
        name=name,
    )(tile_expert, tile_live, xs, g.reshape(1, d), wg, wu, wd)


def _moe_combine_kernel(pos0_ref, pos1_ref, h_ref, route_ref, fn_ref, ys_hbm, o_ref, y0_ref, y1_ref, sem):
    tm = h_ref.shape[0]

    def start(r, c):
        _row_copy(ys_hbm, pos0_ref[0, 0, r], y0_ref, r, sem.at[0]).start()
        _row_copy(ys_hbm, pos1_ref[0, 0, r], y1_ref, r, sem.at[1]).start()
        return c

    def wait(r, c):
        _row_copy(ys_hbm, pos0_ref[0, 0, r], y0_ref, r, sem.at[0]).wait()
        _row_copy(ys_hbm, pos1_ref[0, 0, r], y1_ref, r, sem.at[1]).wait()
        return c

    lax.fori_loop(0, tm, start, 0, unroll=8)
    lax.fori_loop(0, tm, wait, 0, unroll=8)
    route = route_ref[...]
    g1 = route[:, ROUTE_G1:ROUTE_G1 + 1]
    g2 = route[:, ROUTE_G2:ROUTE_G2 + 1]
    o_ref[...] = _rms(h_ref[...] + g1 * y0_ref[...] + g2 * y1_ref[...], fn_ref[...])


def moe_combine(h, route, ys, pos0, pos1, final_g, *, tm=512, name="moe_combine"):
    t, d = h.shape
    idx_spec = pl.BlockSpec((1, 1, tm), lambda i: (i, 0, 0), memory_space=pltpu.SMEM)
    return pl.pallas_call(
        _moe_combine_kernel,
        grid=(t // tm,),
        in_specs=[idx_spec, idx_spec, _row_spec(tm, d), _row_spec(tm, LANES), _full_spec((1, d)),
                  pl.BlockSpec(memory_space=pl.ANY)],
        out_specs=_row_spec(tm, d),
        out_shape=jax.ShapeDtypeStruct((t, d), F32),
        scratch_shapes=[pltpu.VMEM((tm, d), F32), pltpu.VMEM((tm, d), F32), pltpu.SemaphoreType.DMA((2,))],
        compiler_params=_cparams("arbitrary"),
        name=name,
    )(pos0.reshape(t // tm, 1, tm), pos1.reshape(t // tm, 1, tm), h, route, final_g.reshape(1, d), ys)


def moe(h, g, router_pad, wg, wu, wd, final_g, *, tm_rows=1024):
    t, d = h.shape
    n_tiles = 2 * t // tm_rows + N_EXPERTS
    route, counts = moe_route(h, g, router_pad)
    counts = counts[0, :N_EXPERTS].astype(jnp.int32)
    padded = (counts + tm_rows - 1) // tm_rows * tm_rows
    ends = jnp.cumsum(padded)
    offsets = ends - padded
    experts = jnp.arange(N_EXPERTS, dtype=jnp.int32)

    def position(idx_lane, rank_lane):
        e = route[:, idx_lane].astype(jnp.int32)
        off = jnp.sum(jnp.where(e[:, None] == experts[None, :], offsets[None, :], 0), axis=1)
        return off + route[:, rank_lane].astype(jnp.int32)

    pos0, pos1 = position(ROUTE_I1, ROUTE_R1), position(ROUTE_I2, ROUTE_R2)
    tile_start = jnp.arange(n_tiles, dtype=jnp.int32) * tm_rows
    tile_expert = jnp.minimum(jnp.sum(tile_start[:, None] >= ends[None, :], axis=1), N_EXPERTS - 1).astype(jnp.int32)
    tile_live = (tile_start < ends[-1]).astype(jnp.int32)
    xs = moe_dispatch(h, pos0, pos1, n_tiles * tm_rows)
    ys = moe_experts(xs, g, wg, wu, wd, tile_expert, tile_live, tm=tm_rows)
    return moe_combine(h, route, ys, pos0, pos1, final_g)


LOG2E = 1.4426950408889634


def _flash_kernel(qb, kb, q_ref, k_ref, vt_ref, o_ref, acc_ref, sa_ref, sb_ref):
    i = pl.program_id(2)
    key = lax.broadcasted_iota(jnp.int32, (kb, qb), 0)
    qry = lax.broadcasted_iota(jnp.int32, (kb, qb), 1)
    diag_masks = (key <= qry, key + kb <= qry)

    def qk_into(s_ref, blk):
        start = pl.multiple_of(blk * kb, kb)
        for h in range(2):
            k = k_ref[0, pl.ds(start, kb), h * HEAD_LANES:(h + 1) * HEAD_LANES]
            q = q_ref[0, :, h * HEAD_LANES:(h + 1) * HEAD_LANES]
            s_ref[h] = lax.dot_general(k, q, _NT, preferred_element_type=F32)

    def process(s_ref, blk, ms, mask=None):
        new = []
        for h in range(2):
            s = s_ref[h] if mask is None else jnp.where(mask, s_ref[h], NEG_BIG)
            m_new = jnp.maximum(ms[h], jnp.max(s, axis=0, keepdims=True))
            alpha = jnp.exp2(ms[h] - m_new)
            p = jnp.exp2((s - m_new).astype(BF16))
            new.append(m_new)
            acc_ref[h] = alpha * acc_ref[h] + jnp.dot(vt_ref[0, h, blk], p, preferred_element_type=F32)
        return tuple(new)

    acc_ref[...] = jnp.zeros_like(acc_ref)
    qk_into(sa_ref, 0)

    def body(j, ms):
        qk_into(sb_ref, 2 * j + 1)
        ms = process(sa_ref, 2 * j, ms)
        qk_into(sa_ref, 2 * j + 2)
        return process(sb_ref, 2 * j + 1, ms)

    m_init = jnp.full((1, qb), NEG_BIG, F32)
    ms = lax.fori_loop(0, i, body, (m_init, m_init))
    qk_into(sb_ref, 2 * i + 1)
    ms = process(sa_ref, 2 * i, ms, diag_masks[0])
    process(sb_ref, 2 * i + 1, ms, diag_masks[1])
    outs = []
    for h in range(2):
        acc = acc_ref[h]
        outs.append(acc[:HEAD_DIM] / acc[HEAD_DIM:HEAD_DIM + 1])
    o_ref[0] = jnp.concatenate(outs, axis=0).T.astype(o_ref.dtype)


FLASH_V_ROWS = 80


def flash_attention(q, k, v, *, qb=512, name="flash"):
    b, s, hw = q.shape
    n_heads = hw // HEAD_LANES
    kb = qb // 2
    nkb = s // kb
    vt = v.reshape(b, nkb, kb, n_heads, HEAD_DIM).transpose(0, 3, 1, 4, 2)
    pad_rows = FLASH_V_ROWS - HEAD_DIM - 1
    vt = jnp.concatenate([vt, jnp.ones((b, n_heads, nkb, 1, kb), BF16),
                          jnp.zeros((b, n_heads, nkb, pad_rows, kb), BF16)], axis=3)
    return pl.pallas_call(
        functools.partial(_flash_kernel, qb, kb),
        grid=(b, n_heads // 2, s // qb),
        in_specs=[pl.BlockSpec((1, qb, 2 * HEAD_LANES), lambda bi, p, i: (bi, i, p)),
                  pl.BlockSpec((1, s, 2 * HEAD_LANES), lambda bi, p, i: (bi, 0, p)),
                  pl.BlockSpec((1, 2, nkb, FLASH_V_ROWS, kb), lambda bi, p, i: (bi, p, 0, 0, 0))],
        out_specs=pl.BlockSpec((1, qb, 2 * HEAD_DIM), lambda bi, p, i: (bi, i, p)),
        out_shape=jax.ShapeDtypeStruct((b, s, n_heads * HEAD_DIM), BF16),
        scratch_shapes=[pltpu.VMEM((2, FLASH_V_ROWS, qb), F32),
                        pltpu.VMEM((2, kb, qb), F32), pltpu.VMEM((2, kb, qb), F32)],
        compiler_params=_cparams("parallel", "parallel", "arbitrary"),
        name=name,
    )(q, k, vt)


def _rwkv_kernel(prec, x_ref, mu_ref, w0_ref, w2_ref, a0_ref, a2_ref, g2_ref, kk_ref, ka_ref, rk_ref,
                 lng_ref, lnb_ref, o_ref, carry_ref, s_ref):
    c = RW_CHUNK
    gl = RW_GROUP_LANES
    n_groups = GROUP_WIDTH // gl

    @pl.when(pl.program_id(1) == 0)
    def _():
        carry_ref[...] = jnp.zeros_like(carry_ref)
        s_ref[...] = jnp.zeros_like(s_ref)

    nbt = x_ref.shape[0]
    rows = nbt * c
    x = x_ref[...].reshape(rows, x_ref.shape[2])
    row = lax.broadcasted_iota(jnp.int32, x.shape, 0)
    prev = pltpu.roll(x, 1, axis=0)
    for bi in range(nbt):
        prev = jnp.where(row == bi * c, carry_ref[bi:bi + 1, :], prev)
    for bi in range(nbt):
        carry_ref[bi:bi + 1, :] = x[(bi + 1) * c - 1:(bi + 1) * c, :]
    xs = x + mu_ref[...] * (prev - x)

    gw = GROUP_WIDTH
    r, k, v = xs[:, 0:gw], xs[:, gw:2 * gw], xs[:, 2 * gw:3 * gw]
    wa = xs[:, 3 * gw:3 * gw + DECAY_LORA + AAA_LORA]
    gd = xs[:, 3 * gw + DECAY_LORA + AAA_LORA:]
    log_w = -_softplus(-(w0_ref[...] + _dot(jnp.tanh(wa), w2_ref[...], 3))) - 0.5
    lw = -jnp.exp(log_w)
    a = _sigmoid(a0_ref[...] + _dot(wa, a2_ref[...], 3))
    g = _dot(_sigmoid(gd), g2_ref[...], 3)

    brow = lax.broadcasted_iota(jnp.int32, (gl, gl), 0) // HEAD_DIM
    bcol = lax.broadcasted_iota(jnp.int32, (gl, gl), 1) // HEAD_DIM
    bmask = brow == bcol
    bones = jnp.where(bmask, 1.0, 0.0).astype(BF16)

    def head_sum(t):
        return jnp.concatenate([_dot(t[:, i * gl:(i + 1) * gl], bones, 2) for i in range(n_groups)], axis=1)

    kk = k * kk_ref[...]
    kk = kk / jnp.maximum(jnp.sqrt(head_sum(kk * kk)), 1e-12)
    k2 = k * (1.0 + (a - 1.0) * ka_ref[...])
    b = kk * a

    ti = lax.broadcasted_iota(jnp.int32, (rows, rows), 0)
    si = lax.broadcasted_iota(jnp.int32, (rows, rows), 1)
    same_chunk_before = (si <= ti) & (si >= ti // c * c)
    log_p = _dot(jnp.where(same_chunk_before, 1.0, 0.0), lw, 6)
    log_pc = jnp.concatenate([jnp.broadcast_to(log_p[(bi + 1) * c - 1:(bi + 1) * c, :], (c, gw))
                              for bi in range(nbt)], axis=0)
    inv_p = jnp.exp(-log_p)
    to_end = jnp.exp(log_pc - log_p)
    a_t = -kk * jnp.exp(log_p - lw)
    r_t = r * jnp.exp(log_p)
    b_t, k_t = b * inv_p, k2 * inv_p
    b_e, k_e = b * to_end, k2 * to_end
    p_c = jnp.exp(log_pc)

    tl = lax.broadcasted_iota(jnp.int32, (c, gl), 0)
    sl = lax.broadcasted_iota(jnp.int32, (c, gl), 1) % HEAD_DIM
    strict, incl = sl < tl, sl <= tl
    eye_l = jnp.where(sl == tl, 1.0, 0.0)
    er = lax.broadcasted_iota(jnp.int32, (gl, gl), 0)
    ec = lax.broadcasted_iota(jnp.int32, (gl, gl), 1)

    def bd(t):
        return jnp.where(bmask, jnp.concatenate([t] * RW_GROUP_HEADS, axis=0), 0.0)

    groups = range(nbt * n_groups)
    window = lambda n: (slice(n // n_groups * c, (n // n_groups + 1) * c),
                        slice(n % n_groups * gl, (n % n_groups + 1) * gl))
    cut = lambda t: [t[window(n)] for n in groups]
    at, rt, bt, kt, be, ke, vg, pc = (cut(t) for t in (a_t, r_t, b_t, k_t, b_e, k_e, v, p_c))
    gram = [_dot_general(jnp.concatenate([at[gi], rt[gi]], axis=0),
                         jnp.concatenate([bd(bt[gi]), bd(kt[gi])], axis=0), _NT, prec) for gi in groups]
    a_ab = [jnp.where(strict, gram[gi][:c, :gl], 0.0) for gi in groups]
    a_ak = [jnp.where(strict, gram[gi][:c, gl:], 0.0) for gi in groups]
    a_rb = [jnp.where(incl, gram[gi][c:, :gl], 0.0) for gi in groups]
    a_rk = [jnp.where(incl, gram[gi][c:, gl:], 0.0) for gi in groups]
    pw = [_dot(a_ab[gi], bd(a_ab[gi]), prec) for gi in groups]
    tinv = [eye_l + a_ab[gi] for gi in groups]
    akv = [_dot(jnp.concatenate([a_ak[gi], a_rk[gi]], axis=0), bd(vg[gi]), prec) for gi in groups]
    n_sq = int(np.log2(c)) - 1
    for it in range(n_sq):
        if it + 1 < n_sq:
            both = [_dot(jnp.concatenate([tinv[gi], pw[gi]], axis=0), bd(pw[gi]), prec) for gi in groups]
            tinv = [tinv[gi] + both[gi][:c] for gi in groups]
            pw = [both[gi][c:] for gi in groups]
        else:
            tinv = [tinv[gi] + _dot(tinv[gi], bd(pw[gi]), prec) for gi in groups]
    wu = [_dot(tinv[gi], jnp.concatenate([bd(at[gi]), bd(akv[gi][:c])], axis=1), prec) for gi in groups]
    w_g = [wu[gi][:, :gl] for gi in groups]
    u0 = [wu[gi][:, gl:] for gi in groups]
    rwu = [_dot(a_rb[gi], jnp.concatenate([bd(w_g[gi]), bd(u0[gi])], axis=1), prec) for gi in groups]
    m_bd = [jnp.where(bmask, _dot_general(be[gi], w_g[gi], _TN, prec), 0.0)
            + jnp.where(er == ec, pc[gi][0:1, :], 0.0) for gi in groups]
    z_bd = [jnp.where(bmask, _dot_general(jnp.concatenate([ke[gi], be[gi]], axis=0),
                                          jnp.concatenate([vg[gi], u0[gi]], axis=0), _TN, prec), 0.0)
            for gi in groups]
    ys = []
    for gi in groups:
        r_hat = rt[gi] + rwu[gi][:, :gl]
        y0 = akv[gi][c:] + rwu[gi][:, gl:]
        s0 = s_ref[gi]
        ys.append(_dot(r_hat, s0, 3) + y0)
        s_ref[gi] = _dot(m_bd[gi], s0, 3) + z_bd[gi]
    y = jnp.concatenate([jnp.concatenate(ys[bi * n_groups:(bi + 1) * n_groups], axis=1) for bi in range(nbt)],
                        axis=0)

    inv_n = 1.0 / HEAD_DIM
    d = y - head_sum(y) * inv_n
    yn = d * lax.rsqrt(head_sum(d * d) * inv_n + RW_GN_EPS) * lng_ref[...] + lnb_ref[...]
    bonus = head_sum(r * k2 * rk_ref[...]) * v
    o_ref[...] = ((yn + bonus) * g).astype(o_ref.dtype).reshape(o_ref.shape)


def rwkv7(rw, mu, w0, w2, a0, a2, g2, k_k, k_a, r_k, lnx_g, lnx_b, *, prec=1, name="rwkv7"):
    b, s, cols = rw.shape
    gw = GROUP_WIDTH
    zeros = jnp.zeros((DECAY_LORA, gw), F32)
    w2p = jnp.concatenate([w2, zeros], axis=0)
    a2p = jnp.concatenate([zeros, a2], axis=0)
    vec = lambda t: t.reshape(1, -1).astype(F32)
    params = [vec(mu), vec(w0), w2p, vec(a0), a2p, g2, vec(k_k), vec(k_a), vec(r_k), vec(lnx_g), vec(lnx_b)]
    nbt = RW_BATCH if b % RW_BATCH == 0 else 1
    return pl.pallas_call(
        functools.partial(_rwkv_kernel, prec),
        grid=(b // nbt, s // RW_CHUNK),
        in_specs=[pl.BlockSpec((nbt, RW_CHUNK, cols), lambda bi, ci: (bi, ci, 0))]
                 + [pl.BlockSpec(p.shape, lambda bi, ci: (0, 0)) for p in params],
        out_specs=pl.BlockSpec((nbt, RW_CHUNK, gw), lambda bi, ci: (bi, ci, 0)),
        out_shape=jax.ShapeDtypeStruct((b, s, gw), BF16),
        scratch_shapes=[pltpu.VMEM((8, cols), F32),
                        pltpu.VMEM((nbt * gw // RW_GROUP_LANES, RW_GROUP_LANES, RW_GROUP_LANES), F32)],
        compiler_params=_cparams("parallel", "arbitrary"),
        name=name,
    )(rw, *params)


def _tile_lanes(t, n):
    return jnp.concatenate([t] * n, axis=1)


def _expand_matrix():
    e = np.zeros((GROUP_WIDTH, N_HEADS * HEAD_LANES), np.float32)
    for h in range(N_HEADS):
        e[h * HEAD_DIM + np.arange(HEAD_DIM), h * HEAD_LANES + np.arange(HEAD_DIM)] = 1.0
    return jnp.asarray(e, BF16)


def _mla_prep_kernel(x_ref, qn_ref, kvn_ref, wq_ref, wqr_ref, wk_ref, wv_ref,
                     cq_ref, sq_ref, ck_ref, sk_ref, q_ref, k_ref, v_ref):
    x = x_ref[...]
    q_lat = x[:, :MLA_Q_RANK]
    kv_lat = x[:, MLA_Q_RANK:MLA_Q_RANK + MLA_KV_RANK]
    kpe = x[:, MLA_Q_RANK + MLA_KV_RANK:MLA_Q_RANK + MLA_KV_RANK + HEAD_LANES]
    kpe_rot = x[:, MLA_Q_RANK + MLA_KV_RANK + HEAD_LANES:]
    qn = _rms(q_lat, qn_ref[...]).astype(BF16)
    q = (jnp.dot(qn, wq_ref[...], preferred_element_type=F32) * _tile_lanes(cq_ref[...], N_HEADS)
         + jnp.dot(qn, wqr_ref[...], preferred_element_type=F32) * _tile_lanes(sq_ref[...], N_HEADS))
    q_ref[...] = q.astype(BF16)
    kn = _rms(kv_lat, kvn_ref[...]).astype(BF16)
    k_rope = kpe * ck_ref[...] + kpe_rot * sk_ref[...]
    k = jnp.dot(kn, wk_ref[...], preferred_element_type=F32) + _tile_lanes(k_rope, N_HEADS)
    k_ref[...] = k.astype(BF16)
    v_ref[...] = jnp.dot(kn, wv_ref[...], preferred_element_type=F32).astype(BF16)


def mla_prep(x, q_norm, kv_norm, w_uq, w_ukv, seq_len, *, tm=256, name="mla_prep"):
    t = x.shape[0]
    nope, rope, half = HEAD_DIM, MLA_ROPE, MLA_ROPE // 2
    wq = w_uq.reshape(MLA_Q_RANK, N_HEADS, nope + rope)
    wq_pe = wq[:, :, nope:]
    pad = lambda a: jnp.pad(a, ((0, 0), (0, 0), (0, HEAD_LANES - a.shape[2])))
    flat = lambda a: a.reshape(a.shape[0], -1).astype(BF16)
    wq_aug = flat(pad(wq))
    wq_rot = flat(pad(jnp.concatenate([jnp.zeros_like(wq[:, :, :nope]), -wq_pe[:, :, half:], wq_pe[:, :, :half]], axis=2)))
    wkv = w_ukv.reshape(MLA_KV_RANK, N_HEADS, 2 * HEAD_DIM)
    wk_aug = flat(pad(wkv[:, :, :nope]))
    wv = flat(wkv[:, :, nope:])

    scale = float(nope + rope) ** -0.5 * LOG2E
    inv_freq = ROPE_THETA ** (-jnp.arange(0, rope, 2, dtype=F32) / rope)
    ang = jnp.arange(seq_len, dtype=F32)[:, None] * inv_freq[None, :]
    cos2, sin2 = _tile_lanes(jnp.cos(ang), 2), _tile_lanes(jnp.sin(ang), 2)
    z64, z32 = jnp.zeros((seq_len, nope), F32), jnp.zeros((seq_len, HEAD_LANES - nope - rope), F32)
    cq = jnp.concatenate([jnp.full((seq_len, nope), scale, F32), cos2 * scale, z32], axis=1)
    sq = jnp.concatenate([z64, sin2 * scale, z32], axis=1)
    ck = jnp.concatenate([z64, cos2, z32], axis=1)
    sk = jnp.concatenate([z64, sin2, z32], axis=1)

    n_seq_tiles = seq_len // tm
    tab_spec = pl.BlockSpec((tm, HEAD_LANES), lambda i: (i % n_seq_tiles, 0))
    weights = [wq_aug, wq_rot, wk_aug, wv]
    return pl.pallas_call(
        _mla_prep_kernel,
        grid=(t // tm,),
        in_specs=[_row_spec(tm, x.shape[1]), _full_spec((1, MLA_Q_RANK)), _full_spec((1, MLA_KV_RANK))]
                 + [_full_spec(w.shape) for w in weights] + [tab_spec] * 4,
        out_specs=[_row_spec(tm, N_HEADS * HEAD_LANES), _row_spec(tm, N_HEADS * HEAD_LANES),
                   _row_spec(tm, GROUP_WIDTH)],
        out_shape=[jax.ShapeDtypeStruct((t, N_HEADS * HEAD_LANES), BF16),
                   jax.ShapeDtypeStruct((t, N_HEADS * HEAD_LANES), BF16),
                   jax.ShapeDtypeStruct((t, GROUP_WIDTH), BF16)],
        compiler_params=_cparams("parallel"),
        name=name,
    )(x, q_norm.reshape(1, -1), kv_norm.reshape(1, -1), *weights, cq, sq, ck, sk)


FOX_PARTS = 3


def _fox_prep_kernel(scale, q_ref, k_ref, f_ref, bf_ref, e_ref, pq_ref, pk_ref, cq_ref, ck_ref,
                     qo_ref, ko_ref, carry_ref):
    @pl.when(pl.program_id(1) == 0)
    def _():
        carry_ref[...] = jnp.zeros_like(carry_ref)

    tm = q_ref.shape[1]
    log_f = -_softplus(-(f_ref[0] + bf_ref[...]))
    ti = lax.broadcasted_iota(jnp.int32, (tm, tm), 0)
    si = lax.broadcasted_iota(jnp.int32, (tm, tm), 1)
    d = _dot(jnp.where(si <= ti, 1.0, 0.0), log_f, 6) + carry_ref[0:1, :]
    carry_ref[0:1, :] = d[tm - 1:tm, :]
    d2 = d * LOG2E
    hi = d2.astype(BF16)
    rem = d2 - hi.astype(F32)
    mid = rem.astype(BF16)
    lo = (rem - mid.astype(F32)).astype(BF16)
    parts = jnp.concatenate([hi, mid, lo], axis=1)
    q = (jnp.dot((q_ref[0] * scale).astype(BF16), e_ref[...], preferred_element_type=F32)
         + jnp.dot(parts, pq_ref[...], preferred_element_type=F32) + cq_ref[...])
    k = (jnp.dot(k_ref[0].astype(BF16), e_ref[...], preferred_element_type=F32)
         + jnp.dot(parts, pk_ref[...], preferred_element_type=F32) + ck_ref[...])
    qo_ref[0] = q.astype(BF16)
    ko_ref[0] = k.astype(BF16)


def fox_prep(q, k, f_logit, b_f, *, tm=256, name="fox_prep"):
    b, s, _ = q.shape
    width = N_HEADS * HEAD_LANES
    pq = np.zeros((FOX_PARTS * LANES, width), np.float32)
    pk = np.zeros((FOX_PARTS * LANES, width), np.float32)
    cq = np.zeros((1, width), np.float32)
    ck = np.zeros((1, width), np.float32)
    for h in range(N_HEADS):
        base = h * HEAD_LANES + HEAD_DIM
        for p in range(FOX_PARTS):
            pk[p * LANES + h, base + p] = -1.0
            pq[p * LANES + h, base + FOX_PARTS + p] = 1.0
            cq[0, base + p] = 1.0
            ck[0, base + FOX_PARTS + p] = 1.0
    consts = [_expand_matrix(), jnp.asarray(pq, BF16), jnp.asarray(pk, BF16), jnp.asarray(cq), jnp.asarray(ck)]
    bf_pad = jnp.zeros((1, LANES), F32).at[0, :N_HEADS].set(b_f.astype(F32))
    blk = lambda n: pl.BlockSpec((1, tm, n), lambda bi, i: (bi, i, 0))
    return pl.pallas_call(
        functools.partial(_fox_prep_kernel, float(HEAD_DIM) ** -0.5 * LOG2E),
        grid=(b, s // tm),
        in_specs=[blk(GROUP_WIDTH), blk(GROUP_WIDTH), blk(LANES), pl.BlockSpec((1, LANES), lambda bi, i: (0, 0))]
                 + [pl.BlockSpec(c.shape, lambda bi, i: (0, 0)) for c in consts],
        out_specs=[blk(width), blk(width)],
        out_shape=[jax.ShapeDtypeStruct((b, s, width), BF16)] * 2,
        scratch_shapes=[pltpu.VMEM((8, LANES), F32)],
        compiler_params=_cparams("parallel", "arbitrary"),
        name=name,
    )(q, k, f_logit, bf_pad, *consts)


MOBA_NB_PAD = 32


def _rope_full(x, cos, sin):
    half = HEAD_DIM // 2
    lane = lax.broadcasted_iota(jnp.int32, x.shape, 1)
    width = x.shape[1]
    rot = jnp.where(lane % HEAD_DIM < half, -pltpu.roll(x, width - half, axis=1), pltpu.roll(x, half, axis=1))
    return x * cos + rot * sin


def _moba_prep_kernel(q_ref, k_ref, cos_ref, sin_ref, e_ref, qo_ref, ko_ref, km_ref):
    blk = pl.program_id(1)
    cos = _tile_lanes(cos_ref[...], GROUP_WIDTH // LANES)
    sin = _tile_lanes(sin_ref[...], GROUP_WIDTH // LANES)
    qo_ref[0] = _rope_full(q_ref[0], cos, sin)
    k = _rope_full(k_ref[0], cos, sin)
    km_ref[0, 0] = jnp.mean(k, axis=0, keepdims=True)
    lane = lax.broadcasted_iota(jnp.int32, ko_ref.shape[1:], 1) % HEAD_LANES
    own_lane = jnp.where(lane == HEAD_DIM + blk, 1.0, 0.0)
    ko_ref[0] = (jnp.dot(k.astype(BF16), e_ref[...], preferred_element_type=F32) + own_lane).astype(BF16)


def moba_prep(q, k, *, name="moba_prep"):
    b, s, _ = q.shape
    tm = MOBA_BLOCK
    nb = s // tm
    assert nb <= MOBA_NB_PAD
    inv_freq = ROPE_THETA ** (-jnp.arange(0, HEAD_DIM, 2, dtype=F32) / HEAD_DIM)
    ang = jnp.arange(s, dtype=F32)[:, None] * inv_freq[None, :]
    cos, sin = _tile_lanes(jnp.cos(ang), LANES // (HEAD_DIM // 2)), _tile_lanes(jnp.sin(ang), LANES // (HEAD_DIM // 2))
    width = N_HEADS * HEAD_LANES
    blk = lambda n: pl.BlockSpec((1, tm, n), lambda bi, i: (bi, i, 0))
    tab = pl.BlockSpec((tm, LANES), lambda bi, i: (i, 0))
    e = _expand_matrix()
    return pl.pallas_call(
        _moba_prep_kernel,
        grid=(b, nb),
        in_specs=[blk(GROUP_WIDTH), blk(GROUP_WIDTH), tab, tab, pl.BlockSpec(e.shape, lambda bi, i: (0, 0))],
        out_specs=[blk(GROUP_WIDTH), blk(width),
                   pl.BlockSpec((1, 1, 1, GROUP_WIDTH), lambda bi, i: (bi, i, 0, 0))],
        out_shape=[jax.ShapeDtypeStruct((b, s, GROUP_WIDTH), F32), jax.ShapeDtypeStruct((b, s, width), BF16),
                   jax.ShapeDtypeStruct((b, nb, 1, GROUP_WIDTH), F32)],
        compiler_params=_cparams("parallel", "parallel"),
        name=name,
    )(q, k, cos, sin, e)


def _moba_gate_kernel(scale, q_ref, km_ref, e_ref, pm_ref, qo_ref):
    own = pl.program_id(1)
    q = q_ref[0]
    gate_t = _dot_general(km_ref[0], q, _NT, 6)
    nbp = MOBA_NB_PAD
    j = lax.broadcasted_iota(jnp.int32, (nbp, q.shape[0]), 0)
    valid = j < own
    lowest = -3.0e38
    bias_t = []
    for h in range(N_HEADS):
        g = jnp.where(valid, gate_t[h * nbp:(h + 1) * nbp, :], lowest)
        rest = g
        for _ in range(MOBA_TOPK - 1):
            rest = jnp.where(rest >= jnp.max(rest, axis=0, keepdims=True), lowest, rest)
        kth = jnp.max(rest, axis=0, keepdims=True)
        keep = (valid & (g >= kth)) | (j == own)
        bias_t.append(jnp.where(keep, 0.0, NEG_BIG))
    bias = jnp.concatenate(bias_t, axis=0).T
    out = (jnp.dot((q * scale).astype(BF16), e_ref[...], preferred_element_type=F32)
           + jnp.dot(bias.astype(BF16), pm_ref[...], preferred_element_type=F32))
    qo_ref[0] = out.astype(BF16)


def moba_gate(q_rope, k_mean, *, name="moba_gate"):
    b, s, _ = q_rope.shape
    tm = MOBA_BLOCK
    nb = s // tm
    nbp = MOBA_NB_PAD
    width = N_HEADS * HEAD_LANES
    km = k_mean.reshape(b, nb, N_HEADS, HEAD_DIM).transpose(0, 2, 1, 3)
    km = jnp.pad(km, ((0, 0), (0, 0), (0, nbp - nb), (0, 0)))
    eye = jnp.eye(N_HEADS, dtype=F32)
    km_bd = (km[:, :, :, None, :] * eye[None, :, None, :, None]).reshape(b, N_HEADS * nbp, GROUP_WIDTH)
    pm = np.zeros((N_HEADS * nbp, width), np.float32)
    for h in range(N_HEADS):
        pm[h * nbp + np.arange(nbp), h * HEAD_LANES + HEAD_DIM + np.arange(nbp)] = 1.0
    e, pm = _expand_matrix(), jnp.asarray(pm, BF16)
    return pl.pallas_call(
        functools.partial(_moba_gate_kernel, float(HEAD_DIM) ** -0.5 * LOG2E),
        grid=(b, nb),
        in_specs=[pl.BlockSpec((1, tm, GROUP_WIDTH), lambda bi, i: (bi, i, 0)),
                  pl.BlockSpec((1, N_HEADS * nbp, GROUP_WIDTH), lambda bi, i: (bi, 0, 0)),
                  pl.BlockSpec(e.shape, lambda bi, i: (0, 0)), pl.BlockSpec(pm.shape, lambda bi, i: (0, 0))],
        out_specs=pl.BlockSpec((1, tm, width), lambda bi, i: (bi, i, 0)),
        out_shape=jax.ShapeDtypeStruct((b, s, width), BF16),
        compiler_params=_cparams("parallel", "parallel"),
        name=name,
    )(q_rope, km_bd, e, pm)


def kernel(x, norm_mix_0, w_in_0, shift_mu_0, rw_w0_0, rw_w2_0, rw_a0_0, rw_a2_0, rw_g2_0, rw_kk_0, rw_ka_0,
           rw_rk_0, rw_lnx_g_0, rw_lnx_b_0, mla_qnorm_0, mla_wuq_0, mla_kvnorm_0, mla_wukv_0, w_out_0,
           norm_ffn_0, ffn_wg_0, ffn_wu_0, ffn_wd_0, norm_mix_1, w_in_1, fox_bf_1, w_out_1, norm_ffn_1,
           router_1, moe_wg_1, moe_wu_1, moe_wd_1, final_norm):
    b, s, d = x.shape
    t = b * s
    gw = GROUP_WIDTH
    bf = lambda w: w.astype(BF16)
    x2 = x.reshape(t, d)

    mla0 = RW_COLS
    w_q, w_kv = w_in_0[:, mla0:mla0 + MLA_Q_RANK], w_in_0[:, mla0 + MLA_Q_RANK:mla0 + MLA_Q_RANK + MLA_KV_RANK]
    w_kr = w_in_0[:, mla0 + MLA_Q_RANK + MLA_KV_RANK:]
    half = MLA_ROPE // 2
    z = lambda n: jnp.zeros((d, n), F32)
    w_kpe = jnp.concatenate([z(HEAD_DIM), w_kr, z(HEAD_LANES - HEAD_DIM - MLA_ROPE)], axis=1)
    w_kpe_rot = jnp.concatenate([z(HEAD_DIM), -w_kr[:, half:], w_kr[:, :half],
                                 z(HEAD_LANES - HEAD_DIM - MLA_ROPE)], axis=1)
    w_mla = jnp.concatenate([w_q, w_kv, w_kpe, w_kpe_rot], axis=1)
    rw, mla = norm_proj(x2, norm_mix_0, [bf(w_in_0[:, :RW_COLS]), bf(w_mla)], [F32, F32], name="in_proj_0")
    y_a = rwkv7(rw.reshape(b, s, RW_COLS), shift_mu_0, rw_w0_0, rw_w2_0, rw_a0_0, rw_a2_0, rw_g2_0,
                rw_kk_0, rw_ka_0, rw_rk_0, rw_lnx_g_0, rw_lnx_b_0)
    q, k, v = mla_prep(mla, mla_qnorm_0, mla_kvnorm_0, mla_wuq_0, mla_wukv_0, s)
    y_b = flash_attention(q.reshape(b, s, -1), k.reshape(b, s, -1), v.reshape(b, s, -1), name="flash_mla")
    h = out_proj(y_a.reshape(t, gw), y_b.reshape(t, gw), bf(w_out_0[:gw]), bf(w_out_0[gw:]), x2, name="out_proj_0")
    h = ffn(h, norm_ffn_0, bf(ffn_wg_0), bf(ffn_wu_0), bf(ffn_wd_0))

    c0 = 3 * gw
    w_f = jnp.pad(w_in_1[:, c0:c0 + N_HEADS], ((0, 0), (0, LANES - N_HEADS)))
    c1 = c0 + N_HEADS
    cols = [w_in_1[:, 0:gw], w_in_1[:, gw:2 * gw], w_in_1[:, 2 * gw:3 * gw], w_f,
            w_in_1[:, c1:c1 + gw], w_in_1[:, c1 + gw:c1 + 2 * gw], w_in_1[:, c1 + 2 * gw:]]
    fq, fk, fv, ff, mq, mk, mv = norm_proj(h, norm_mix_1, [bf(w) for w in cols],
                                           [F32, F32, BF16, F32, F32, F32, BF16], name="in_proj_1")
    r3 = lambda a: a.reshape(b, s, -1)
    fqa, fka = fox_prep(r3(fq), r3(fk), r3(ff), fox_bf_1)
    y_c = flash_attention(fqa, fka, r3(fv), name="flash_fox")
    mq_rope, mka, k_mean = moba_prep(r3(mq), r3(mk))
    mqa = moba_gate(mq_rope, k_mean)
    y_d = flash_attention(mqa, mka, r3(mv), name="flash_moba")
    h = out_proj(y_c.reshape(t, gw), y_d.reshape(t, gw), bf(w_out_1[:gw]), bf(w_out_1[gw:]), h, name="out_proj_1")

    router_pad = jnp.pad(router_1, ((0, 0), (0, LANES - N_EXPERTS)))
    out = moe(h, norm_ffn_1, router_pad, bf(moe_wg_1), bf(moe_wu_1), bf(moe_wd_1), final_norm)
    return out.reshape(b, s, d)
```

```python
import functools

import numpy as np
import jax
import jax.numpy as jnp
from jax import lax
from jax.experimental import pallas as pl
from jax.experimental.pallas import tpu as pltpu

F32 = jnp.float32
BF16 = jnp.bfloat16

HEAD_DIM = 64
N_HEADS = 8
GROUP_WIDTH = N_HEADS * HEAD_DIM
HEAD_LANES = 128
MLA_ROPE = 32
MLA_Q_RANK = 256
MLA_KV_RANK = 128
DECAY_LORA = 64
AAA_LORA = 64
GATE_LORA = 128
RW_COLS = 3 * GROUP_WIDTH + DECAY_LORA + AAA_LORA + GATE_LORA
RW_GN_EPS = 64e-5
MOBA_BLOCK = 256
MOBA_TOPK = 3
ROPE_THETA = 10000.0
NORM_EPS = 1e-6
N_EXPERTS = 8
NEG_BIG = -1e30

LANES = 128
VMEM_LIMIT_BYTES = 56 * 1024 * 1024

RW_CHUNK = 64
RW_GROUP_HEADS = 4
RW_GROUP_LANES = RW_GROUP_HEADS * HEAD_DIM
RW_BATCH = 4


def _cparams(*semantics):
    return pltpu.CompilerParams(dimension_semantics=semantics, vmem_limit_bytes=VMEM_LIMIT_BYTES)


def _dot(a, b, prec=1):
    if prec == 6:
        return jnp.dot(a.astype(F32), b.astype(F32), preferred_element_type=F32,
                       precision=lax.Precision.HIGHEST)
    if prec == 1:
        return jnp.dot(a.astype(BF16), b.astype(BF16), preferred_element_type=F32)
    ah, al = _split(a)
    if b.dtype == BF16:
        return jnp.dot(ah, b, preferred_element_type=F32) + jnp.dot(al, b, preferred_element_type=F32)
    bh, bl = _split(b)
    return (jnp.dot(ah, bh, preferred_element_type=F32) + jnp.dot(ah, bl, preferred_element_type=F32)
            + jnp.dot(al, bh, preferred_element_type=F32))


def _split(x):
    hi = x.astype(BF16)
    return hi, (x - hi.astype(F32)).astype(BF16)


_NT = (((1,), (1,)), ((), ()))
_TN = (((0,), (0,)), ((), ()))


def _dot_general(a, b, dims, prec=1):
    if prec == 6:
        return lax.dot_general(a.astype(F32), b.astype(F32), dims, preferred_element_type=F32,
                               precision=lax.Precision.HIGHEST)
    if prec == 1:
        return lax.dot_general(a.astype(BF16), b.astype(BF16), dims, preferred_element_type=F32)
    ah, al = _split(a)
    bh, bl = _split(b)
    return (lax.dot_general(ah, bh, dims, preferred_element_type=F32)
            + lax.dot_general(ah, bl, dims, preferred_element_type=F32)
            + lax.dot_general(al, bh, dims, preferred_element_type=F32))


def _rms(x, g):
    return x * lax.rsqrt(jnp.mean(x * x, axis=-1, keepdims=True) + NORM_EPS) * g


def _softplus(x):
    return jnp.maximum(x, 0.0) + jnp.log(1.0 + jnp.exp(-jnp.abs(x)))


def _sigmoid(x):
    return 1.0 / (1.0 + jnp.exp(-x))


def _row_spec(tm, n):
    return pl.BlockSpec((tm, n), lambda i: (i, 0))


def _full_spec(shape):
    nd = len(shape)
    return pl.BlockSpec(shape, lambda *_: (0,) * nd)


def _norm_proj_kernel(n_out, x_ref, g_ref, *refs):
    w_refs, o_refs = refs[:n_out], refs[n_out:]
    xn = _rms(x_ref[...], g_ref[...]).astype(BF16)
    for w_ref, o_ref in zip(w_refs, o_refs):
        o_ref[...] = jnp.dot(xn, w_ref[...], preferred_element_type=F32).astype(o_ref.dtype)


def norm_proj(x, g, weights, dtypes, *, tm=256, name="norm_proj"):
    t, d = x.shape
    n_out = len(weights)
    return pl.pallas_call(
        functools.partial(_norm_proj_kernel, n_out),
        grid=(t // tm,),
        in_specs=[_row_spec(tm, d), _full_spec((1, d))] + [_full_spec(w.shape) for w in weights],
        out_specs=[_row_spec(tm, w.shape[1]) for w in weights],
        out_shape=[jax.ShapeDtypeStruct((t, w.shape[1]), dt) for w, dt in zip(weights, dtypes)],
        compiler_params=_cparams("parallel"),
        name=name,
    )(x, g.reshape(1, d), *weights)


def _out_proj_kernel(y1_ref, y2_ref, w1_ref, w2_ref, r_ref, o_ref):
    o_ref[...] = (r_ref[...] + jnp.dot(y1_ref[...], w1_ref[...], preferred_element_type=F32)
                  + jnp.dot(y2_ref[...], w2_ref[...], preferred_element_type=F32))


def out_proj(y1, y2, w1, w2, resid, *, tm=512, name="out_proj"):
    t, d = resid.shape
    return pl.pallas_call(
        _out_proj_kernel,
        grid=(t // tm,),
        in_specs=[_row_spec(tm, y1.shape[1]), _row_spec(tm, y2.shape[1]),
                  _full_spec(w1.shape), _full_spec(w2.shape), _row_spec(tm, d)],
        out_specs=_row_spec(tm, d),
        out_shape=jax.ShapeDtypeStruct((t, d), F32),
        compiler_params=_cparams("parallel"),
        name=name,
    )(y1, y2, w1, w2, resid)


def _ffn_kernel(h_ref, g_ref, wg_ref, wu_ref, wd_ref, o_ref, xn_ref, acc_ref):
    f = pl.program_id(1)

    @pl.when(f == 0)
    def _():
        xn_ref[...] = _rms(h_ref[...], g_ref[...]).astype(BF16)
        acc_ref[...] = jnp.zeros_like(acc_ref)

    xn = xn_ref[...]
    gate = jnp.dot(xn, wg_ref[...], preferred_element_type=F32)
    up = jnp.dot(xn, wu_ref[...], preferred_element_type=F32)
    act = (gate * _sigmoid(gate) * up).astype(BF16)
    acc_ref[...] += jnp.dot(act, wd_ref[...], preferred_element_type=F32)

    @pl.when(f == pl.num_programs(1) - 1)
    def _():
        o_ref[...] = h_ref[...] + acc_ref[...]


def ffn(h, g, wg, wu, wd, *, tm=1024, tf=256, name="ffn"):
    t, d = h.shape
    dff = wg.shape[1]
    return pl.pallas_call(
        _ffn_kernel,
        grid=(t // tm, dff // tf),
        in_specs=[pl.BlockSpec((tm, d), lambda i, f: (i, 0)),
                  pl.BlockSpec((1, d), lambda i, f: (0, 0)),
                  pl.BlockSpec((d, tf), lambda i, f: (0, f)),
                  pl.BlockSpec((d, tf), lambda i, f: (0, f)),
                  pl.BlockSpec((tf, d), lambda i, f: (f, 0))],
        out_specs=pl.BlockSpec((tm, d), lambda i, f: (i, 0)),
        out_shape=jax.ShapeDtypeStruct((t, d), F32),
        scratch_shapes=[pltpu.VMEM((tm, d), BF16), pltpu.VMEM((tm, d), F32)],
        compiler_params=_cparams("parallel", "arbitrary"),
        name=name,
    )(h, g.reshape(1, d), wg, wu, wd)


ROUTE_I1, ROUTE_I2, ROUTE_G1, ROUTE_G2, ROUTE_R1, ROUTE_R2 = range(6)


def _moe_route_kernel(h_ref, g_ref, router_ref, route_ref, counts_ref, carry_ref):
    @pl.when(pl.program_id(0) == 0)
    def _():
        carry_ref[...] = jnp.zeros_like(carry_ref)

    tm = h_ref.shape[0]
    lane = lax.broadcasted_iota(jnp.int32, (tm, LANES), 1)
    t = _rms(h_ref[...], g_ref[...])
    logits = _dot(t, router_ref[...], prec=6)
    logits = jnp.where(lane < N_EXPERTS, logits, NEG_BIG)
    m1 = jnp.max(logits, axis=-1, keepdims=True)
    i1 = jnp.min(jnp.where(logits == m1, lane, LANES), axis=-1, keepdims=True)
    rest = jnp.where(lane == i1, NEG_BIG, logits)
    m2 = jnp.max(rest, axis=-1, keepdims=True)
    i2 = jnp.min(jnp.where(rest == m2, lane, LANES), axis=-1, keepdims=True)
    ex = jnp.exp(m2 - m1)
    g1 = 1.0 / (1.0 + ex)
    sel = jnp.where((lane == i1) | (lane == i2), 1.0, 0.0)
    ti = lax.broadcasted_iota(jnp.int32, (tm, tm), 0)
    si = lax.broadcasted_iota(jnp.int32, (tm, tm), 1)
    before = jnp.where(si < ti, 1.0, 0.0).astype(BF16)
    rank = jnp.dot(before, sel.astype(BF16), preferred_element_type=F32) + carry_ref[0:1, :]
    carry_ref[0:1, :] = carry_ref[0:1, :] + jnp.sum(sel, axis=0, keepdims=True)
    r1 = jnp.sum(jnp.where(lane == i1, rank, 0.0), axis=-1, keepdims=True)
    r2 = jnp.sum(jnp.where(lane == i2, rank, 0.0), axis=-1, keepdims=True)
    fields = [i1.astype(F32), i2.astype(F32), g1, ex * g1, r1, r2]
    route = jnp.zeros((tm, LANES), F32)
    for idx, val in enumerate(fields):
        route = jnp.where(lane == idx, val, route)
    route_ref[...] = route
    counts_ref[...] = carry_ref[...]


def moe_route(h, g, router_pad, *, tm=512, name="moe_route"):
    t, d = h.shape
    return pl.pallas_call(
        _moe_route_kernel,
        grid=(t // tm,),
        in_specs=[_row_spec(tm, d), _full_spec((1, d)), _full_spec((d, LANES))],
        out_specs=[_row_spec(tm, LANES), _full_spec((8, LANES))],
        out_shape=[jax.ShapeDtypeStruct((t, LANES), F32), jax.ShapeDtypeStruct((8, LANES), F32)],
        scratch_shapes=[pltpu.VMEM((8, LANES), F32)],
        compiler_params=_cparams("arbitrary"),
        name=name,
    )(h, g.reshape(1, d), router_pad)


def _row_copy(src_hbm, src_row, dst, dst_row, sem):
    return pltpu.make_async_copy(src_hbm.at[pl.ds(src_row, 1)], dst.at[pl.ds(dst_row, 1)], sem)


def _moe_dispatch_kernel(pos0_ref, pos1_ref, h_ref, init_hbm, xs_hbm, sem):
    del init_hbm
    tm = h_ref.shape[0]

    def start(r, c):
        _row_copy(h_ref, r, xs_hbm, pos0_ref[0, 0, r], sem.at[0]).start()
        _row_copy(h_ref, r, xs_hbm, pos1_ref[0, 0, r], sem.at[1]).start()
        return c

    def wait(r, c):
        _row_copy(h_ref, r, xs_hbm, pos0_ref[0, 0, r], sem.at[0]).wait()
        _row_copy(h_ref, r, xs_hbm, pos1_ref[0, 0, r], sem.at[1]).wait()
        return c

    lax.fori_loop(0, tm, start, 0, unroll=8)
    lax.fori_loop(0, tm, wait, 0, unroll=8)


def moe_dispatch(h, pos0, pos1, n_rows, *, tm=1024, name="moe_dispatch"):
    t, d = h.shape
    idx_spec = pl.BlockSpec((1, 1, tm), lambda i: (i, 0, 0), memory_space=pltpu.SMEM)
    any_spec = pl.BlockSpec(memory_space=pl.ANY)
    return pl.pallas_call(
        _moe_dispatch_kernel,
        grid=(t // tm,),
        in_specs=[idx_spec, idx_spec, _row_spec(tm, d), any_spec],
        out_specs=any_spec,
        out_shape=jax.ShapeDtypeStruct((n_rows, d), F32),
        scratch_shapes=[pltpu.SemaphoreType.DMA((2,))],
        input_output_aliases={3: 0},
        compiler_params=_cparams("arbitrary"),
        name=name,
    )(pos0.reshape(t // tm, 1, tm), pos1.reshape(t // tm, 1, tm), h, jnp.zeros((n_rows, d), F32))


def _moe_expert_kernel(te_ref, tv_ref, x_ref, g_ref, wg_ref, wu_ref, wd_ref, o_ref, xn_ref, acc_ref):
    i = pl.program_id(0)
    f = pl.program_id(1)
    last = pl.num_programs(1) - 1
    live = tv_ref[i] == 1

    @pl.when(live & (f == 0))
    def _():
        xn_ref[...] = _rms(x_ref[...], g_ref[...]).astype(BF16)
        acc_ref[...] = jnp.zeros_like(acc_ref)

    @pl.when(live)
    def _():
        xn = xn_ref[...]
        gate = jnp.dot(xn, wg_ref[0], preferred_element_type=F32)
        up = jnp.dot(xn, wu_ref[0], preferred_element_type=F32)
        act = (gate * _sigmoid(gate) * up).astype(BF16)
        acc_ref[...] += jnp.dot(act, wd_ref[0], preferred_element_type=F32)

    @pl.when(live & (f == last))
    def _():
        o_ref[...] = acc_ref[...]

    @pl.when(jnp.logical_not(live) & (f == last))
    def _():
        o_ref[...] = jnp.zeros_like(o_ref)


def moe_experts(xs, g, wg, wu, wd, tile_expert, tile_live, *, tm, tf=896, name="moe_experts"):
    n_rows, d = xs.shape
    dff = wg.shape[2]
    nf = dff // tf
    fidx = lambda i, f, te, tv: f * tv[i] + (nf - 1) * (1 - tv[i])
    grid_spec = pltpu.PrefetchScalarGridSpec(
        num_scalar_prefetch=2,
        grid=(n_rows // tm, nf),
        in_specs=[pl.BlockSpec((tm, d), lambda i, f, te, tv: (i, 0)),
                  pl.BlockSpec((1, d), lambda i, f, te, tv: (0, 0)),
                  pl.BlockSpec((1, d, tf), lambda i, f, te, tv: (te[i], 0, fidx(i, f, te, tv))),
                  pl.BlockSpec((1, d, tf), lambda i, f, te, tv: (te[i], 0, fidx(i, f, te, tv))),
                  pl.BlockSpec((1, tf, d), lambda i, f, te, tv: (te[i], fidx(i, f, te, tv), 0))],
        out_specs=pl.BlockSpec((tm, d), lambda i, f, te, tv: (i, 0)),
        scratch_shapes=[pltpu.VMEM((tm, d), BF16), pltpu.VMEM((tm, d), F32)])
    return pl.pallas_call(
        _moe_expert_kernel,
        grid_spec=grid_spec,
        out_shape=jax.ShapeDtypeStruct((n_rows, d), F32),
        compiler_params=_cparams("arbitrary", "arbitrary"),
        name=name,
    )(tile_expert, tile_live, xs, g.reshape(1, d), wg, wu, wd)


def _moe_combine_kernel(pos0_ref, pos1_ref, h_ref, route_ref, fn_ref, ys_hbm, o_ref, y0_ref, y1_ref, sem):
    tm = h_ref.shape[0]

    def start(r, c):
        _row_copy(ys_hbm, pos0_ref[0, 0, r], y0_ref, r, sem.at[0]).start()
        _row_copy(ys_hbm, pos1_ref[0, 0, r], y1_ref, r, sem.at[1]).start()
        return c

    def wait(r, c):
        _row_copy(ys_hbm, pos0_ref[0, 0, r], y0_ref, r, sem.at[0]).wait()
        _row_copy(ys_hbm, pos1_ref[0, 0, r], y1_ref, r, sem.at[1]).wait()
        return c

    lax.fori_loop(0, tm, start, 0, unroll=8)
    lax.fori_loop(0, tm, wait, 0, unroll=8)
    route = route_ref[...]
    g1 = route[:, ROUTE_G1:ROUTE_G1 + 1]
    g2 = route[:, ROUTE_G2:ROUTE_G2 + 1]
    o_ref[...] = _rms(h_ref[...] + g1 * y0_ref[...] + g2 * y1_ref[...], fn_ref[...])


def moe_combine(h, route, ys, pos0, pos1, final_g, *, tm=512, name="moe_combine"):
    t, d = h.shape
    idx_spec = pl.BlockSpec((1, 1, tm), lambda i: (i, 0, 0), memory_space=pltpu.SMEM)
    return pl.pallas_call(
        _moe_combine_kernel,
        grid=(t // tm,),
        in_specs=[idx_spec, idx_spec, _row_spec(tm, d), _row_spec(tm, LANES), _full_spec((1, d)),
                  pl.BlockSpec(memory_space=pl.ANY)],
        out_specs=_row_spec(tm, d),
        out_shape=jax.ShapeDtypeStruct((t, d), F32),
        scratch_shapes=[pltpu.VMEM((tm, d), F32), pltpu.VMEM((tm, d), F32), pltpu.SemaphoreType.DMA((2,))],
        compiler_params=_cparams("arbitrary"),
        name=name,
    )(pos0.reshape(t // tm, 1, tm), pos1.reshape(t // tm, 1, tm), h, route, final_g.reshape(1, d), ys)


def moe(h, g, router_pad, wg, wu, wd, final_g, *, tm_rows=1024):
    t, d = h.shape
    n_tiles = 2 * t // tm_rows + N_EXPERTS
    route, counts = moe_route(h, g, router_pad)
    counts = counts[0, :N_EXPERTS].astype(jnp.int32)
    padded = (counts + tm_rows - 1) // tm_rows * tm_rows
    ends = jnp.cumsum(padded)
    offsets = ends - padded
    experts = jnp.arange(N_EXPERTS, dtype=jnp.int32)

    def position(idx_lane, rank_lane):
        e = route[:, idx_lane].astype(jnp.int32)
        off = jnp.sum(jnp.where(e[:, None] == experts[None, :], offsets[None, :], 0), axis=1)
        return off + route[:, rank_lane].astype(jnp.int32)

    pos0, pos1 = position(ROUTE_I1, ROUTE_R1), position(ROUTE_I2, ROUTE_R2)
    tile_start = jnp.arange(n_tiles, dtype=jnp.int32) * tm_rows
    tile_expert = jnp.minimum(jnp.sum(tile_start[:, None] >= ends[None, :], axis=1), N_EXPERTS - 1).astype(jnp.int32)
    tile_live = (tile_start < ends[-1]).astype(jnp.int32)
    xs = moe_dispatch(h, pos0, pos1, n_tiles * tm_rows)
    ys = moe_experts(xs, g, wg, wu, wd, tile_expert, tile_live, tm=tm_rows)
    return moe_combine(h, route, ys, pos0, pos1, final_g)


LOG2E = 1.4426950408889634


def _flash_kernel(qb, kb, q_ref, k_ref, vt_ref, o_ref, acc_ref, sa_ref, sb_ref):
    i = pl.program_id(2)
    hp = sa_ref.shape[0]
    key = lax.broadcasted_iota(jnp.int32, (kb, qb), 0)
    qry = lax.broadcasted_iota(jnp.int32, (kb, qb), 1)
    diag_masks = (key <= qry, key + kb <= qry)

    def qk_into(s_ref, blk):
        start = pl.multiple_of(blk * kb, kb)
        for h in range(hp):
            k = k_ref[0, pl.ds(start, kb), h * HEAD_LANES:(h + 1) * HEAD_LANES]
            q = q_ref[0, :, h * HEAD_LANES:(h + 1) * HEAD_LANES]
            s_ref[h] = lax.dot_general(k, q, _NT, preferred_element_type=F32)

    def process(s_ref, blk, ms, mask=None):
        new = []
        for h in range(hp):
            s = s_ref[h] if mask is None else jnp.where(mask, s_ref[h], NEG_BIG)
            m_new = jnp.maximum(ms[h], jnp.max(s, axis=0, keepdims=True))
            alpha = jnp.exp2(ms[h] - m_new)
            p = jnp.exp2((s - m_new).astype(BF16))
            new.append(m_new)
            acc_ref[h] = alpha * acc_ref[h] + jnp.dot(vt_ref[0, h, blk], p, preferred_element_type=F32)
        return tuple(new)

    acc_ref[...] = jnp.zeros_like(acc_ref)
    qk_into(sa_ref, 0)

    def body(j, ms):
        qk_into(sb_ref, 2 * j + 1)
        ms = process(sa_ref, 2 * j, ms)
        qk_into(sa_ref, 2 * j + 2)
        return process(sb_ref, 2 * j + 1, ms)

    m_init = jnp.full((1, qb), NEG_BIG, F32)
    ms = lax.fori_loop(0, i, body, (m_init,) * hp)
    qk_into(sb_ref, 2 * i + 1)
    ms = process(sa_ref, 2 * i, ms, diag_masks[0])
    process(sb_ref, 2 * i + 1, ms, diag_masks[1])
    outs = []
    for h in range(hp):
        acc = acc_ref[h]
        outs.append(acc[:HEAD_DIM] / acc[HEAD_DIM:HEAD_DIM + 1])
    o_ref[0] = jnp.concatenate(outs, axis=0).T.astype(o_ref.dtype)


FLASH_HEADS = 4
FLASH_V_ROWS = 80


def flash_attention(q, k, v, *, qb=512, hp=FLASH_HEADS, name="flash"):
    b, s, hw = q.shape
    n_heads = hw // HEAD_LANES
    kb = qb // 2
    nkb = s // kb
    vt = v.reshape(b, nkb, kb, n_heads, HEAD_DIM).transpose(0, 3, 1, 4, 2)
    pad_rows = FLASH_V_ROWS - HEAD_DIM - 1
    vt = jnp.concatenate([vt, jnp.ones((b, n_heads, nkb, 1, kb), BF16),
                          jnp.zeros((b, n_heads, nkb, pad_rows, kb), BF16)], axis=3)
    return pl.pallas_call(
        functools.partial(_flash_kernel, qb, kb),
        grid=(b, n_heads // hp, s // qb),
        in_specs=[pl.BlockSpec((1, qb, hp * HEAD_LANES), lambda bi, p, i: (bi, i, p)),
                  pl.BlockSpec((1, s, hp * HEAD_LANES), lambda bi, p, i: (bi, 0, p)),
                  pl.BlockSpec((1, hp, nkb, FLASH_V_ROWS, kb), lambda bi, p, i: (bi, p, 0, 0, 0))],
        out_specs=pl.BlockSpec((1, qb, hp * HEAD_DIM), lambda bi, p, i: (bi, i, p)),
        out_shape=jax.ShapeDtypeStruct((b, s, n_heads * HEAD_DIM), BF16),
        scratch_shapes=[pltpu.VMEM((hp, FLASH_V_ROWS, qb), F32),
                        pltpu.VMEM((hp, kb, qb), F32), pltpu.VMEM((hp, kb, qb), F32)],
        compiler_params=_cparams("parallel", "parallel", "arbitrary"),
        name=name,
    )(q, k, vt)


def _rwkv_kernel(prec, x_ref, mu_ref, w0_ref, w2_ref, a0_ref, a2_ref, g2_ref, kk_ref, ka_ref, rk_ref,
                 lng_ref, lnb_ref, o_ref, carry_ref, s_ref):
    c = RW_CHUNK
    gl = RW_GROUP_LANES
    n_groups = GROUP_WIDTH // gl

    @pl.when(pl.program_id(1) == 0)
    def _():
        carry_ref[...] = jnp.zeros_like(carry_ref)
        s_ref[...] = jnp.zeros_like(s_ref)

    nbt = x_ref.shape[0]
    rows = nbt * c
    x = x_ref[...].reshape(rows, x_ref.shape[2])
    row = lax.broadcasted_iota(jnp.int32, x.shape, 0)
    prev = pltpu.roll(x, 1, axis=0)
    for bi in range(nbt):
        prev = jnp.where(row == bi * c, carry_ref[bi:bi + 1, :], prev)
    for bi in range(nbt):
        carry_ref[bi:bi + 1, :] = x[(bi + 1) * c - 1:(bi + 1) * c, :]
    xs = x + mu_ref[...] * (prev - x)

    gw = GROUP_WIDTH
    r, k, v = xs[:, 0:gw], xs[:, gw:2 * gw], xs[:, 2 * gw:3 * gw]
    wa = xs[:, 3 * gw:3 * gw + DECAY_LORA + AAA_LORA]
    gd = xs[:, 3 * gw + DECAY_LORA + AAA_LORA:]
    log_w = -_softplus(-(w0_ref[...] + _dot(jnp.tanh(wa), w2_ref[...], 3))) - 0.5
    lw = -jnp.exp(log_w)
    a = _sigmoid(a0_ref[...] + _dot(wa, a2_ref[...], 3))
    g = _dot(_sigmoid(gd), g2_ref[...], 3)

    brow = lax.broadcasted_iota(jnp.int32, (gl, gl), 0) // HEAD_DIM
    bcol = lax.broadcasted_iota(jnp.int32, (gl, gl), 1) // HEAD_DIM
    bmask = brow == bcol
    bones = jnp.where(bmask, 1.0, 0.0).astype(BF16)

    def head_sum(t):
        return jnp.concatenate([_dot(t[:, i * gl:(i + 1) * gl], bones, 2) for i in range(n_groups)], axis=1)

    kk = k * kk_ref[...]
    kk = kk / jnp.maximum(jnp.sqrt(head_sum(kk * kk)), 1e-12)
    k2 = k * (1.0 + (a - 1.0) * ka_ref[...])
    b = kk * a

    ti = lax.broadcasted_iota(jnp.int32, (rows, rows), 0)
    si = lax.broadcasted_iota(jnp.int32, (rows, rows), 1)
    same_chunk_before = (si <= ti) & (si >= ti // c * c)
    log_p = _dot(jnp.where(same_chunk_before, 1.0, 0.0), lw, 6)
    log_pc = jnp.concatenate([jnp.broadcast_to(log_p[(bi + 1) * c - 1:(bi + 1) * c, :], (c, gw))
                              for bi in range(nbt)], axis=0)
    inv_p = jnp.exp(-log_p)
    to_end = jnp.exp(log_pc - log_p)
    a_t = -kk * jnp.exp(log_p - lw)
    r_t = r * jnp.exp(log_p)
    b_t, k_t = b * inv_p, k2 * inv_p
    b_e, k_e = b * to_end, k2 * to_end
    p_c = jnp.exp(log_pc)

    tl = lax.broadcasted_iota(jnp.int32, (c, gl), 0)
    sl = lax.broadcasted_iota(jnp.int32, (c, gl), 1) % HEAD_DIM
    strict, incl = sl < tl, sl <= tl
    eye_l = jnp.where(sl == tl, 1.0, 0.0)
    er = lax.broadcasted_iota(jnp.int32, (gl, gl), 0)
    ec = lax.broadcasted_iota(jnp.int32, (gl, gl), 1)

    def bd(t):
        return jnp.where(bmask, jnp.concatenate([t] * RW_GROUP_HEADS, axis=0), 0.0)

    groups = range(nbt * n_groups)
    window = lambda n: (slice(n // n_groups * c, (n // n_groups + 1) * c),
                        slice(n % n_groups * gl, (n % n_groups + 1) * gl))
    cut = lambda t: [t[window(n)] for n in groups]
    at, rt, bt, kt, be, ke, vg, pc = (cut(t) for t in (a_t, r_t, b_t, k_t, b_e, k_e, v, p_c))
    gram = [_dot_general(jnp.concatenate([at[gi], rt[gi]], axis=0),
                         jnp.concatenate([bd(bt[gi]), bd(kt[gi])], axis=0), _NT, prec) for gi in groups]
    a_ab = [jnp.where(strict, gram[gi][:c, :gl], 0.0) for gi in groups]
    a_ak = [jnp.where(strict, gram[gi][:c, gl:], 0.0) for gi in groups]
    a_rb = [jnp.where(incl, gram[gi][c:, :gl], 0.0) for gi in groups]
    a_rk = [jnp.where(incl, gram[gi][c:, gl:], 0.0) for gi in groups]
    pw = [_dot(a_ab[gi], bd(a_ab[gi]), prec) for gi in groups]
    tinv = [eye_l + a_ab[gi] for gi in groups]
    akv = [_dot(jnp.concatenate([a_ak[gi], a_rk[gi]], axis=0), bd(vg[gi]), prec) for gi in groups]
    n_sq = int(np.log2(c)) - 1
    for it in range(n_sq):
        if it + 1 < n_sq:
            both = [_dot(jnp.concatenate([tinv[gi], pw[gi]], axis=0), bd(pw[gi]), prec) for gi in groups]
            tinv = [tinv[gi] + both[gi][:c] for gi in groups]
            pw = [both[gi][c:] for gi in groups]
        else:
            tinv = [tinv[gi] + _dot(tinv[gi], bd(pw[gi]), prec) for gi in groups]
    wu = [_dot(tinv[gi], jnp.concatenate([bd(at[gi]), bd(akv[gi][:c])], axis=1), prec) for gi in groups]
    w_g = [wu[gi][:, :gl] for gi in groups]
    u0 = [wu[gi][:, gl:] for gi in groups]
    rwu = [_dot(a_rb[gi], jnp.concatenate([bd(w_g[gi]), bd(u0[gi])], axis=1), prec) for gi in groups]
    m_bd = [jnp.where(bmask, _dot_general(be[gi], w_g[gi], _TN, prec), 0.0)
            + jnp.where(er == ec, pc[gi][0:1, :], 0.0) for gi in groups]
    z_bd = [jnp.where(bmask, _dot_general(jnp.concatenate([ke[gi], be[gi]], axis=0),
                                          jnp.concatenate([vg[gi], u0[gi]], axis=0), _TN, prec), 0.0)
            for gi in groups]
    ys = []
    for gi in groups:
        r_hat = rt[gi] + rwu[gi][:, :gl]
        y0 = akv[gi][c:] + rwu[gi][:, gl:]
        s0 = s_ref[gi]
        ys.append(_dot(r_hat, s0, 3) + y0)
        s_ref[gi] = _dot(m_bd[gi], s0, 3) + z_bd[gi]
    y = jnp.concatenate([jnp.concatenate(ys[bi * n_groups:(bi + 1) * n_groups], axis=1) for bi in range(nbt)],
                        axis=0)

    inv_n = 1.0 / HEAD_DIM
    d = y - head_sum(y) * inv_n
    yn = d * lax.rsqrt(head_sum(d * d) * inv_n + RW_GN_EPS) * lng_ref[...] + lnb_ref[...]
    bonus = head_sum(r * k2 * rk_ref[...]) * v
    o_ref[...] = ((yn + bonus) * g).astype(o_ref.dtype).reshape(o_ref.shape)


def rwkv7(rw, mu, w0, w2, a0, a2, g2, k_k, k_a, r_k, lnx_g, lnx_b, *, prec=1, name="rwkv7"):
    b, s, cols = rw.shape
    gw = GROUP_WIDTH
    zeros = jnp.zeros((DECAY_LORA, gw), F32)
    w2p = jnp.concatenate([w2, zeros], axis=0)
    a2p = jnp.concatenate([zeros, a2], axis=0)
    vec = lambda t: t.reshape(1, -1).astype(F32)
    params = [vec(mu), vec(w0), w2p, vec(a0), a2p, g2, vec(k_k), vec(k_a), vec(r_k), vec(lnx_g), vec(lnx_b)]
    nbt = RW_BATCH if b % RW_BATCH == 0 else 1
    return pl.pallas_call(
        functools.partial(_rwkv_kernel, prec),
        grid=(b // nbt, s // RW_CHUNK),
        in_specs=[pl.BlockSpec((nbt, RW_CHUNK, cols), lambda bi, ci: (bi, ci, 0))]
                 + [pl.BlockSpec(p.shape, lambda bi, ci: (0, 0)) for p in params],
        out_specs=pl.BlockSpec((nbt, RW_CHUNK, gw), lambda bi, ci: (bi, ci, 0)),
        out_shape=jax.ShapeDtypeStruct((b, s, gw), BF16),
        scratch_shapes=[pltpu.VMEM((8, cols), F32),
                        pltpu.VMEM((nbt * gw // RW_GROUP_LANES, RW_GROUP_LANES, RW_GROUP_LANES), F32)],
        compiler_params=_cparams("parallel", "arbitrary"),
        name=name,
    )(rw, *params)


def _tile_lanes(t, n):
    return jnp.concatenate([t] * n, axis=1)


def _expand_matrix():
    e = np.zeros((GROUP_WIDTH, N_HEADS * HEAD_LANES), np.float32)
    for h in range(N_HEADS):
        e[h * HEAD_DIM + np.arange(HEAD_DIM), h * HEAD_LANES + np.arange(HEAD_DIM)] = 1.0
    return jnp.asarray(e, BF16)


def _mla_prep_kernel(x_ref, qn_ref, kvn_ref, wq_ref, wqr_ref, wk_ref, wv_ref,
                     cq_ref, sq_ref, ck_ref, sk_ref, q_ref, k_ref, v_ref):
    x = x_ref[...]
    q_lat = x[:, :MLA_Q_RANK]
    kv_lat = x[:, MLA_Q_RANK:MLA_Q_RANK + MLA_KV_RANK]
    kpe = x[:, MLA_Q_RANK + MLA_KV_RANK:MLA_Q_RANK + MLA_KV_RANK + HEAD_LANES]
    kpe_rot = x[:, MLA_Q_RANK + MLA_KV_RANK + HEAD_LANES:]
    qn = _rms(q_lat, qn_ref[...]).astype(BF16)
    q = (jnp.dot(qn, wq_ref[...], preferred_element_type=F32) * _tile_lanes(cq_ref[...], N_HEADS)
         + jnp.dot(qn, wqr_ref[...], preferred_element_type=F32) * _tile_lanes(sq_ref[...], N_HEADS))
    q_ref[...] = q.astype(BF16)
    kn = _rms(kv_lat, kvn_ref[...]).astype(BF16)
    k_rope = kpe * ck_ref[...] + kpe_rot * sk_ref[...]
    k = jnp.dot(kn, wk_ref[...], preferred_element_type=F32) + _tile_lanes(k_rope, N_HEADS)
    k_ref[...] = k.astype(BF16)
    v_ref[...] = jnp.dot(kn, wv_ref[...], preferred_element_type=F32).astype(BF16)


def mla_prep(x, q_norm, kv_norm, w_uq, w_ukv, seq_len, *, tm=256, name="mla_prep"):
    t = x.shape[0]
    nope, rope, half = HEAD_DIM, MLA_ROPE, MLA_ROPE // 2
    wq = w_uq.reshape(MLA_Q_RANK, N_HEADS, nope + rope)
    wq_pe = wq[:, :, nope:]
    pad = lambda a: jnp.pad(a, ((0, 0), (0, 0), (0, HEAD_LANES - a.shape[2])))
    flat = lambda a: a.reshape(a.shape[0], -1).astype(BF16)
    wq_aug = flat(pad(wq))
    wq_rot = flat(pad(jnp.concatenate([jnp.zeros_like(wq[:, :, :nope]), -wq_pe[:, :, half:], wq_pe[:, :, :half]], axis=2)))
    wkv = w_ukv.reshape(MLA_KV_RANK, N_HEADS, 2 * HEAD_DIM)
    wk_aug = flat(pad(wkv[:, :, :nope]))
    wv = flat(wkv[:, :, nope:])

    scale = float(nope + rope) ** -0.5 * LOG2E
    inv_freq = ROPE_THETA ** (-jnp.arange(0, rope, 2, dtype=F32) / rope)
    ang = jnp.arange(seq_len, dtype=F32)[:, None] * inv_freq[None, :]
    cos2, sin2 = _tile_lanes(jnp.cos(ang), 2), _tile_lanes(jnp.sin(ang), 2)
    z64, z32 = jnp.zeros((seq_len, nope), F32), jnp.zeros((seq_len, HEAD_LANES - nope - rope), F32)
    cq = jnp.concatenate([jnp.full((seq_len, nope), scale, F32), cos2 * scale, z32], axis=1)
    sq = jnp.concatenate([z64, sin2 * scale, z32], axis=1)
    ck = jnp.concatenate([z64, cos2, z32], axis=1)
    sk = jnp.concatenate([z64, sin2, z32], axis=1)

    n_seq_tiles = seq_len // tm
    tab_spec = pl.BlockSpec((tm, HEAD_LANES), lambda i: (i % n_seq_tiles, 0))
    weights = [wq_aug, wq_rot, wk_aug, wv]
    return pl.pallas_call(
        _mla_prep_kernel,
        grid=(t // tm,),
        in_specs=[_row_spec(tm, x.shape[1]), _full_spec((1, MLA_Q_RANK)), _full_spec((1, MLA_KV_RANK))]
                 + [_full_spec(w.shape) for w in weights] + [tab_spec] * 4,
        out_specs=[_row_spec(tm, N_HEADS * HEAD_LANES), _row_spec(tm, N_HEADS * HEAD_LANES),
                   _row_spec(tm, GROUP_WIDTH)],
        out_shape=[jax.ShapeDtypeStruct((t, N_HEADS * HEAD_LANES), BF16),
                   jax.ShapeDtypeStruct((t, N_HEADS * HEAD_LANES), BF16),
                   jax.ShapeDtypeStruct((t, GROUP_WIDTH), BF16)],
        compiler_params=_cparams("parallel"),
        name=name,
    )(x, q_norm.reshape(1, -1), kv_norm.reshape(1, -1), *weights, cq, sq, ck, sk)


FOX_PARTS = 3


def _fox_prep_kernel(scale, q_ref, k_ref, f_ref, bf_ref, e_ref, pq_ref, pk_ref, cq_ref, ck_ref,
                     qo_ref, ko_ref, carry_ref):
    @pl.when(pl.program_id(1) == 0)
    def _():
        carry_ref[...] = jnp.zeros_like(carry_ref)

    tm = q_ref.shape[1]
    log_f = -_softplus(-(f_ref[0] + bf_ref[...]))
    ti = lax.broadcasted_iota(jnp.int32, (tm, tm), 0)
    si = lax.broadcasted_iota(jnp.int32, (tm, tm), 1)
    d = _dot(jnp.where(si <= ti, 1.0, 0.0), log_f, 6) + carry_ref[0:1, :]
    carry_ref[0:1, :] = d[tm - 1:tm, :]
    d2 = d * LOG2E
    hi = d2.astype(BF16)
    rem = d2 - hi.astype(F32)
    mid = rem.astype(BF16)
    lo = (rem - mid.astype(F32)).astype(BF16)
    parts = jnp.concatenate([hi, mid, lo], axis=1)
    q = (jnp.dot((q_ref[0] * scale).astype(BF16), e_ref[...], preferred_element_type=F32)
         + jnp.dot(parts, pq_ref[...], preferred_element_type=F32) + cq_ref[...])
    k = (jnp.dot(k_ref[0].astype(BF16), e_ref[...], preferred_element_type=F32)
         + jnp.dot(parts, pk_ref[...], preferred_element_type=F32) + ck_ref[...])
    qo_ref[0] = q.astype(BF16)
    ko_ref[0] = k.astype(BF16)


def fox_prep(q, k, f_logit, b_f, *, tm=256, name="fox_prep"):
    b, s, _ = q.shape
    width = N_HEADS * HEAD_LANES
    pq = np.zeros((FOX_PARTS * LANES, width), np.float32)
    pk = np.zeros((FOX_PARTS * LANES, width), np.float32)
    cq = np.zeros((1, width), np.float32)
    ck = np.zeros((1, width), np.float32)
    for h in range(N_HEADS):
        base = h * HEAD_LANES + HEAD_DIM
        for p in range(FOX_PARTS):
            pk[p * LANES + h, base + p] = -1.0
            pq[p * LANES + h, base + FOX_PARTS + p] = 1.0
            cq[0, base + p] = 1.0
            ck[0, base + FOX_PARTS + p] = 1.0
    consts = [_expand_matrix(), jnp.asarray(pq, BF16), jnp.asarray(pk, BF16), jnp.asarray(cq), jnp.asarray(ck)]
    bf_pad = jnp.zeros((1, LANES), F32).at[0, :N_HEADS].set(b_f.astype(F32))
    blk = lambda n: pl.BlockSpec((1, tm, n), lambda bi, i: (bi, i, 0))
    return pl.pallas_call(
        functools.partial(_fox_prep_kernel, float(HEAD_DIM) ** -0.5 * LOG2E),
        grid=(b, s // tm),
        in_specs=[blk(GROUP_WIDTH), blk(GROUP_WIDTH), blk(LANES), pl.BlockSpec((1, LANES), lambda bi, i: (0, 0))]
                 + [pl.BlockSpec(c.shape, lambda bi, i: (0, 0)) for c in consts],
        out_specs=[blk(width), blk(width)],
        out_shape=[jax.ShapeDtypeStruct((b, s, width), BF16)] * 2,
        scratch_shapes=[pltpu.VMEM((8, LANES), F32)],
        compiler_params=_cparams("parallel", "arbitrary"),
        name=name,
    )(q, k, f_logit, bf_pad, *consts)


MOBA_NB_PAD = 32


def _rope_full(x, cos, sin):
    half = HEAD_DIM // 2
    lane = lax.broadcasted_iota(jnp.int32, x.shape, 1)
    width = x.shape[1]
    rot = jnp.where(lane % HEAD_DIM < half, -pltpu.roll(x, width - half, axis=1), pltpu.roll(x, half, axis=1))
    return x * cos + rot * sin


def _moba_prep_kernel(q_ref, k_ref, cos_ref, sin_ref, e_ref, qo_ref, ko_ref, km_ref):
    blk = pl.program_id(1)
    cos = _tile_lanes(cos_ref[...], GROUP_WIDTH // LANES)
    sin = _tile_lanes(sin_ref[...], GROUP_WIDTH // LANES)
    qo_ref[0] = _rope_full(q_ref[0], cos, sin)
    k = _rope_full(k_ref[0], cos, sin)
    km_ref[0, 0] = jnp.mean(k, axis=0, keepdims=True)
    lane = lax.broadcasted_iota(jnp.int32, ko_ref.shape[1:], 1) % HEAD_LANES
    own_lane = jnp.where(lane == HEAD_DIM + blk, 1.0, 0.0)
    ko_ref[0] = (jnp.dot(k.astype(BF16), e_ref[...], preferred_element_type=F32) + own_lane).astype(BF16)


def moba_prep(q, k, *, name="moba_prep"):
    b, s, _ = q.shape
    tm = MOBA_BLOCK
    nb = s // tm
    assert nb <= MOBA_NB_PAD
    inv_freq = ROPE_THETA ** (-jnp.arange(0, HEAD_DIM, 2, dtype=F32) / HEAD_DIM)
    ang = jnp.arange(s, dtype=F32)[:, None] * inv_freq[None, :]
    cos, sin = _tile_lanes(jnp.cos(ang), LANES // (HEAD_DIM // 2)), _tile_lanes(jnp.sin(ang), LANES // (HEAD_DIM // 2))
    width = N_HEADS * HEAD_LANES
    blk = lambda n: pl.BlockSpec((1, tm, n), lambda bi, i: (bi, i, 0))
    tab = pl.BlockSpec((tm, LANES), lambda bi, i: (i, 0))
    e = _expand_matrix()
    return pl.pallas_call(
        _moba_prep_kernel,
        grid=(b, nb),
        in_specs=[blk(GROUP_WIDTH), blk(GROUP_WIDTH), tab, tab, pl.BlockSpec(e.shape, lambda bi, i: (0, 0))],
        out_specs=[blk(GROUP_WIDTH), blk(width),
                   pl.BlockSpec((1, 1, 1, GROUP_WIDTH), lambda bi, i: (bi, i, 0, 0))],
        out_shape=[jax.ShapeDtypeStruct((b, s, GROUP_WIDTH), F32), jax.ShapeDtypeStruct((b, s, width), BF16),
                   jax.ShapeDtypeStruct((b, nb, 1, GROUP_WIDTH), F32)],
        compiler_params=_cparams("parallel", "parallel"),
        name=name,
    )(q, k, cos, sin, e)


def _moba_gate_kernel(scale, q_ref, km_ref, e_ref, pm_ref, qo_ref):
    own = pl.program_id(1)
    q = q_ref[0]
    gate_t = _dot_general(km_ref[0], q, _NT, 6)
    nbp = MOBA_NB_PAD
    j = lax.broadcasted_iota(jnp.int32, (nbp, q.shape[0]), 0)
    valid = j < own
    lowest = -3.0e38
    bias_t = []
    for h in range(N_HEADS):
        g = jnp.where(valid, gate_t[h * nbp:(h + 1) * nbp, :], lowest)
        rest = g
        for _ in range(MOBA_TOPK - 1):
            rest = jnp.where(rest >= jnp.max(rest, axis=0, keepdims=True), lowest, rest)
        kth = jnp.max(rest, axis=0, keepdims=True)
        keep = (valid & (g >= kth)) | (j == own)
        bias_t.append(jnp.where(keep, 0.0, NEG_BIG))
    bias = jnp.concatenate(bias_t, axis=0).T
    out = (jnp.dot((q * scale).astype(BF16), e_ref[...], preferred_element_type=F32)
           + jnp.dot(bias.astype(BF16), pm_ref[...], preferred_element_type=F32))
    qo_ref[0] = out.astype(BF16)


def moba_gate(q_rope, k_mean, *, name="moba_gate"):
    b, s, _ = q_rope.shape
    tm = MOBA_BLOCK
    nb = s // tm
    nbp = MOBA_NB_PAD
    width = N_HEADS * HEAD_LANES
    km = k_mean.reshape(b, nb, N_HEADS, HEAD_DIM).transpose(0, 2, 1, 3)
    km = jnp.pad(km, ((0, 0), (0, 0), (0, nbp - nb), (0, 0)))
    eye = jnp.eye(N_HEADS, dtype=F32)
    km_bd = (km[:, :, :, None, :] * eye[None, :, None, :, None]).reshape(b, N_HEADS * nbp, GROUP_WIDTH)
    pm = np.zeros((N_HEADS * nbp, width), np.float32)
    for h in range(N_HEADS):
        pm[h * nbp + np.arange(nbp), h * HEAD_LANES + HEAD_DIM + np.arange(nbp)] = 1.0
    e, pm = _expand_matrix(), jnp.asarray(pm, BF16)
    return pl.pallas_call(
        functools.partial(_moba_gate_kernel, float(HEAD_DIM) ** -0.5 * LOG2E),
        grid=(b, nb),
        in_specs=[pl.BlockSpec((1, tm, GROUP_WIDTH), lambda bi, i: (bi, i, 0)),
                  pl.BlockSpec((1, N_HEADS * nbp, GROUP_WIDTH), lambda bi, i: (bi, 0, 0)),
                  pl.BlockSpec(e.shape, lambda bi, i: (0, 0)), pl.BlockSpec(pm.shape, lambda bi, i: (0, 0))],
        out_specs=pl.BlockSpec((1, tm, width), lambda bi, i: (bi, i, 0)),
        out_shape=jax.ShapeDtypeStruct((b, s, width), BF16),
        compiler_params=_cparams("parallel", "parallel"),
        name=name,
    )(q_rope, km_bd, e, pm)


def kernel(x, norm_mix_0, w_in_0, shift_mu_0, rw_w0_0, rw_w2_0, rw_a0_0, rw_a2_0, rw_g2_0, rw_kk_0, rw_ka_0,
           rw_rk_0, rw_lnx_g_0, rw_lnx_b_0, mla_qnorm_0, mla_wuq_0, mla_kvnorm_0, mla_wukv_0, w_out_0,
           norm_ffn_0, ffn_wg_0, ffn_wu_0, ffn_wd_0, norm_mix_1, w_in_1, fox_bf_1, w_out_1, norm_ffn_1,
           router_1, moe_wg_1, moe_wu_1, moe_wd_1, final_norm):
    b, s, d = x.shape
    t = b * s
    gw = GROUP_WIDTH
    bf = lambda w: w.astype(BF16)
    x2 = x.reshape(t, d)

    mla0 = RW_COLS
    w_q, w_kv = w_in_0[:, mla0:mla0 + MLA_Q_RANK], w_in_0[:, mla0 + MLA_Q_RANK:mla0 + MLA_Q_RANK + MLA_KV_RANK]
    w_kr = w_in_0[:, mla0 + MLA_Q_RANK + MLA_KV_RANK:]
    half = MLA_ROPE // 2
    z = lambda n: jnp.zeros((d, n), F32)
    w_kpe = jnp.concatenate([z(HEAD_DIM), w_kr, z(HEAD_LANES - HEAD_DIM - MLA_ROPE)], axis=1)
    w_kpe_rot = jnp.concatenate([z(HEAD_DIM), -w_kr[:, half:], w_kr[:, :half],
                                 z(HEAD_LANES - HEAD_DIM - MLA_ROPE)], axis=1)
    w_mla = jnp.concatenate([w_q, w_kv, w_kpe, w_kpe_rot], axis=1)
    rw, mla = norm_proj(x2, norm_mix_0, [bf(w_in_0[:, :RW_COLS]), bf(w_mla)], [F32, F32], name="in_proj_0")
    y_a = rwkv7(rw.reshape(b, s, RW_COLS), shift_mu_0, rw_w0_0, rw_w2_0, rw_a0_0, rw_a2_0, rw_g2_0,
                rw_kk_0, rw_ka_0, rw_rk_0, rw_lnx_g_0, rw_lnx_b_0)
    q, k, v = mla_prep(mla, mla_qnorm_0, mla_kvnorm_0, mla_wuq_0, mla_wukv_0, s)
    y_b = flash_attention(q.reshape(b, s, -1), k.reshape(b, s, -1), v.reshape(b, s, -1), name="flash_mla")
    h = out_proj(y_a.reshape(t, gw), y_b.reshape(t, gw), bf(w_out_0[:gw]), bf(w_out_0[gw:]), x2, name="out_proj_0")
    h = ffn(h, norm_ffn_0, bf(ffn_wg_0), bf(ffn_wu_0), bf(ffn_wd_0))

    c0 = 3 * gw
    w_f = jnp.pad(w_in_1[:, c0:c0 + N_HEADS], ((0, 0), (0, LANES - N_HEADS)))
    c1 = c0 + N_HEADS
    cols = [w_in_1[:, 0:gw], w_in_1[:, gw:2 * gw], w_in_1[:, 2 * gw:3 * gw], w_f,
            w_in_1[:, c1:c1 + gw], w_in_1[:, c1 + gw:c1 + 2 * gw], w_in_1[:, c1 + 2 * gw:]]
    fq, fk, fv, ff, mq, mk, mv = norm_proj(h, norm_mix_1, [bf(w) for w in cols],
                                           [F32, F32, BF16, F32, F32, F32, BF16], name="in_proj_1")
    r3 = lambda a: a.reshape(b, s, -1)
    fqa, fka = fox_prep(r3(fq), r3(fk), r3(ff), fox_bf_1)
    y_c = flash_attention(fqa, fka, r3(fv), name="flash_fox")
    mq_rope, mka, k_mean = moba_prep(r3(mq), r3(mk))
    mqa = moba_gate(mq_rope, k_mean)
    y_d = flash_attention(mqa, mka, r3(mv), name="flash_moba")
    h = out_proj(y_c.reshape(t, gw), y_d.reshape(t, gw), bf(w_out_1[:gw]), bf(w_out_1[gw:]), h, name="out_proj_1")

    router_pad = jnp.pad(router_1, ((0, 0), (0, LANES - N_EXPERTS)))
    out = moe(h, norm_ffn_1, router_pad, bf(moe_wg_1), bf(moe_wu_1), bf(moe_wd_1), final_norm)
    return out.reshape(b, s, d)
```
